```python
import math
import jax, jax.numpy as jnp
from jax import lax
import numpy as np

D_MODEL = 1024
BATCH = 1
SEQ = 16384
DEPTH = 2

ATT_HEADS = 8
ATT_KV_HEADS = 2
ATT_HEAD_DIM = 64
ATT_BLOCK = 128
ATT_WINDOW = 128
CONV_WIDTH = 512
CONV_KSIZE = 3
GLA_HEADS = 4
GLA_KEY_DIM = 256
GLA_VALUE_DIM = 512
GLA_GATE_RANK = 16
GLA_TAU = 16.0
GLA_CHUNK = 16
N_BRANCHES = 3
N_EXPERTS = 16
EXPERT_FF = 2048
CAPACITY_FACTOR = 2
DN_ALPHA = (2 * DEPTH) ** 0.25
DN_BETA = (8 * DEPTH) ** -0.25
LN_EPS = 1e-5
RMS_EPS = 1e-6

SPLIT_SIZES = (
    ATT_HEADS * ATT_HEAD_DIM,
    ATT_KV_HEADS * ATT_HEAD_DIM,
    ATT_KV_HEADS * ATT_HEAD_DIM,
    CONV_WIDTH,
    CONV_WIDTH,
    CONV_WIDTH,
    GLA_KEY_DIM,
    GLA_KEY_DIM,
    GLA_VALUE_DIM,
    GLA_VALUE_DIM,
    2 * GLA_GATE_RANK,
    N_BRANCHES * D_MODEL,
)
N_IN = int(sum(SPLIT_SIZES))
SPLIT_POINTS = [int(v) for v in np.cumsum(SPLIT_SIZES)[:-1]]

kernel_name = "hybrid_swa_conv_gla_ec_moe_deepnorm"


def layer_norm(x, g, b):
    xf = x.astype(jnp.float32)
    mu = jnp.mean(xf, axis=-1, keepdims=True)
    var = jnp.mean(jnp.square(xf - mu), axis=-1, keepdims=True)
    return ((xf - mu) * lax.rsqrt(var + LN_EPS) * g.astype(jnp.float32) + b.astype(jnp.float32)).astype(x.dtype)


def window_attention(q, k, v, sink):
    B, S = q.shape[0], q.shape[1]
    nb = S // ATT_BLOCK
    G = ATT_HEADS // ATT_KV_HEADS
    qb = q.reshape(B, nb, ATT_BLOCK, ATT_KV_HEADS, G, ATT_HEAD_DIM)

    def neighbourhood(t):
        tp = jnp.pad(t, ((0, 0), (ATT_BLOCK, ATT_BLOCK), (0, 0), (0, 0)))
        tp = tp.reshape(B, nb + 2, ATT_BLOCK, ATT_KV_HEADS, ATT_HEAD_DIM)
        return jnp.concatenate([tp[:, :-2], tp[:, 1:-1], tp[:, 2:]], axis=2)

    kw, vw = neighbourhood(k), neighbourhood(v)
    s = jnp.einsum('bnqhgd,bnkhd->bnhgqk', qb, kw).astype(jnp.float32) * (ATT_HEAD_DIM ** -0.5)
    q_off = jnp.arange(ATT_BLOCK)
    k_off = jnp.arange(3 * ATT_BLOCK) - ATT_BLOCK
    dist = jnp.abs(q_off[:, None] - k_off[None, :]).astype(jnp.float32)
    k_abs = jnp.arange(nb)[:, None] * ATT_BLOCK + k_off[None, :]
    valid = (dist <= ATT_WINDOW)[None] & ((k_abs >= 0) & (k_abs < S))[:, None, :]
    slopes = (2.0 ** (-8.0 * jnp.arange(1, ATT_HEADS + 1, dtype=jnp.float32) / ATT_HEADS)).reshape(ATT_KV_HEADS, G)
    s = s - slopes[:, :, None, None] * dist
    s = jnp.where(valid[None, :, None, None], s, -jnp.inf)
    sink_l = sink.astype(jnp.float32).reshape(ATT_KV_HEADS, G)[:, :, None, None]
    m = jnp.maximum(jnp.max(s, axis=-1, keepdims=True), sink_l)
    p = jnp.exp(s - m)
    p = p / (jnp.sum(p, axis=-1, keepdims=True) + jnp.exp(sink_l - m))
    o = jnp.einsum('bnhgqk,bnkhd->bnqhgd', p.astype(v.dtype), vw)
    return o.reshape(B, S, ATT_HEADS * ATT_HEAD_DIM)


def short_conv(h, b_gate, c_gate, w):
    u = c_gate * h
    y = lax.conv_general_dilated(
        u, w[:, None, :].astype(u.dtype), window_strides=(1,), padding=[(CONV_KSIZE // 2, CONV_KSIZE // 2)],
        dimension_numbers=('NWC', 'WIO', 'NWC'), feature_group_count=CONV_WIDTH)
    return b_gate * y


def gla_direction(q, k, v, log_a):
    f32 = jnp.float32
    B, S, H, dk = q.shape
    dv = v.shape[-1]
    C = GLA_CHUNK
    nc = S // C
    qc = q.astype(f32).reshape(B, nc, C, H, dk)
    kc = k.astype(f32).reshape(B, nc, C, H, dk)
    vc = v.astype(f32).reshape(B, nc, C, H, dv)
    b = jnp.cumsum(log_a.astype(f32).reshape(B, nc, C, H, dk), axis=2)
    lower = jnp.tril(jnp.ones((C, C), dtype=bool))[:, :, None, None]
    diff = b[:, :, :, None] - b[:, :, None, :]
    decay = jnp.exp(jnp.where(lower, diff, -jnp.inf))
    scores = jnp.einsum('bnthd,bnshd,bntshd->bnhts', qc, kc, decay)
    o_intra = jnp.einsum('bnhts,bnshv->bnthv', scores, vc)
    b_last = b[:, :, -1]
    kv = jnp.einsum('bnshd,bnshv->bnhdv', kc * jnp.exp(b_last[:, :, None] - b), vc)
    a_chunk = jnp.exp(b_last)

    def step(state, inp):
        a_n, kv_n = inp
        return a_n[..., None] * state + kv_n, state

    _, states = lax.scan(step, jnp.zeros((B, H, dk, dv), f32),
                         (jnp.moveaxis(a_chunk, 1, 0), jnp.moveaxis(kv, 1, 0)))
    states = jnp.moveaxis(states, 0, 1)
    o_inter = jnp.einsum('bnthd,bnhdv->bnthv', qc * jnp.exp(b), states)
    return (o_intra + o_inter).reshape(B, S, H, dv)


def gla_mixer(q, k, v, r, lr, w2, bias, norm_g):
    B, S = q.shape[0], q.shape[1]
    dk = GLA_KEY_DIM // GLA_HEADS
    dv = GLA_VALUE_DIM // GLA_HEADS
    qh = q.reshape(B, S, GLA_HEADS, dk) * (dk ** -0.5)
    kh = k.reshape(B, S, GLA_HEADS, dk)
    vh = v.reshape(B, S, GLA_HEADS, dv)
    z = jnp.einsum('bsir,irk->bsik', lr.reshape(B, S, 2, GLA_GATE_RANK), w2) + bias
    log_a = (jax.nn.log_sigmoid(z.astype(jnp.float32)) / GLA_TAU).reshape(B, S, 2, GLA_HEADS, dk)
    flip = lambda t: jnp.flip(t, axis=1)
    o_f = gla_direction(qh, kh, vh, log_a[:, :, 0])
    o_b = flip(gla_direction(flip(qh), flip(kh), flip(vh), flip(log_a[:, :, 1])))
    o = o_f + o_b
    o = o * lax.rsqrt(jnp.mean(o * o, axis=-1, keepdims=True) + RMS_EPS)
    o = o.reshape(B, S, GLA_VALUE_DIM) * norm_g.astype(jnp.float32)
    return (o * jax.nn.silu(r.astype(jnp.float32))).astype(r.dtype)


def expert_choice_ffn(x, router_w, w_gate, w_up, w_down):
    B, S, D = x.shape
    cap = CAPACITY_FACTOR * S // N_EXPERTS
    aff = jax.nn.softmax(jnp.einsum('bsd,de->bse', x, router_w).astype(jnp.float32), axis=-1)
    gate, idx = lax.top_k(jnp.swapaxes(aff, 1, 2), cap)
    xs = jax.vmap(lambda xb, ib: xb[ib])(x, idx)
    h = jax.nn.silu(jnp.einsum('becd,edf->becf', xs, w_gate)) * jnp.einsum('becd,edf->becf', xs, w_up)
    y = jnp.einsum('becf,efd->becd', h, w_down) * gate[..., None].astype(x.dtype)
    return jax.vmap(lambda ib, yb: jnp.zeros((S, D), yb.dtype).at[ib.reshape(-1)].add(yb.reshape(-1, D)))(idx, y)


def setup_inputs(seed: int = 0) -> dict:
    key = jax.random.key(seed)
    ks = jax.random.split(key, 20)
    L, D = DEPTH, D_MODEL
    nrm = lambda k, shape, scale: jax.random.normal(k, shape, jnp.float32) * scale
    return {
        "x": nrm(ks[0], (BATCH, SEQ, D), 1.0),
        "w_in": nrm(ks[1], (L, D, N_IN), D ** -0.5),
        "attn_sink": nrm(ks[2], (L, ATT_HEADS), 0.5),
        "conv_w": nrm(ks[3], (L, CONV_KSIZE, CONV_WIDTH), CONV_KSIZE ** -0.5),
        "gla_gate_w2": nrm(ks[4], (L, 2, GLA_GATE_RANK, GLA_KEY_DIM), GLA_GATE_RANK ** -0.5),
        "gla_gate_b": nrm(ks[5], (L, 2, GLA_KEY_DIM), 0.1),
        "gla_norm_g": 1.0 + nrm(ks[6], (L, GLA_VALUE_DIM), 0.02),
        "w_branch_attn": nrm(ks[7], (L, ATT_HEADS * ATT_HEAD_DIM, D), DN_BETA * (ATT_HEADS * ATT_HEAD_DIM) ** -0.5),
        "w_branch_conv": nrm(ks[8], (L, CONV_WIDTH, D), DN_BETA * CONV_WIDTH ** -0.5),
        "w_branch_gla": nrm(ks[9], (L, GLA_VALUE_DIM, D), DN_BETA * GLA_VALUE_DIM ** -0.5),
        "w_out": nrm(ks[10], (L, D, D), DN_BETA * D ** -0.5),
        "ln_mix_g": 1.0 + nrm(ks[11], (L, D), 0.02),
        "ln_mix_b": nrm(ks[12], (L, D), 0.02),
        "router_w": nrm(ks[13], (L, D, N_EXPERTS), D ** -0.5),
        "expert_w_gate": nrm(ks[14], (L, N_EXPERTS, D, EXPERT_FF), D ** -0.5),
        "expert_w_up": nrm(ks[15], (L, N_EXPERTS, D, EXPERT_FF), D ** -0.5),
        "expert_w_down": nrm(ks[16], (L, N_EXPERTS, EXPERT_FF, D), DN_BETA * EXPERT_FF ** -0.5),
        "ln_ffn_g": 1.0 + nrm(ks[17], (L, D), 0.02),
        "ln_ffn_b": nrm(ks[18], (L, D), 0.02),
    }


def reference(x, w_in, attn_sink, conv_w, gla_gate_w2, gla_gate_b, gla_norm_g,
              w_branch_attn, w_branch_conv, w_branch_gla, w_out, ln_mix_g, ln_mix_b,
              router_w, expert_w_gate, expert_w_up, expert_w_down, ln_ffn_g, ln_ffn_b):
    B, S, _ = x.shape
    for l in range(DEPTH):
        proj = jnp.einsum('bsd,dn->bsn', x, w_in[l])
        (aq, ak, av, ch, cb, cc, gq, gk, gv, gr, glr, mg) = jnp.split(proj, SPLIT_POINTS, axis=-1)
        y_attn = window_attention(aq.reshape(B, S, ATT_HEADS, ATT_HEAD_DIM),
                                  ak.reshape(B, S, ATT_KV_HEADS, ATT_HEAD_DIM),
                                  av.reshape(B, S, ATT_KV_HEADS, ATT_HEAD_DIM), attn_sink[l])
        y_conv = short_conv(ch, cb, cc, conv_w[l])
        y_gla = gla_mixer(gq, gk, gv, gr, glr, gla_gate_w2[l], gla_gate_b[l], gla_norm_g[l])
        g = jax.nn.sigmoid(mg.reshape(B, S, N_BRANCHES, D_MODEL))
        merged = (g[:, :, 0] * (y_attn @ w_branch_attn[l])
                  + g[:, :, 1] * (y_conv @ w_branch_conv[l])
                  + g[:, :, 2] * (y_gla @ w_branch_gla[l]))
        x = layer_norm(DN_ALPHA * x + merged @ w_out[l], ln_mix_g[l], ln_mix_b[l])
        ffn = expert_choice_ffn(x, router_w[l], expert_w_gate[l], expert_w_up[l], expert_w_down[l])
        x = layer_norm(DN_ALPHA * x + ffn, ln_ffn_g[l], ln_ffn_b[l])
    return x
```

```python
import functools
import math

import numpy as np
import jax
import jax.numpy as jnp
from jax import lax
from jax.experimental import pallas as pl
from jax.experimental.pallas import tpu as pltpu

F32 = jnp.float32
BF16 = jnp.bfloat16
I32 = jnp.int32

D_MODEL = 1024
DEPTH = 2
ATT_HEADS, ATT_KV_HEADS, ATT_HEAD_DIM, ATT_BLOCK, ATT_WINDOW = 8, 2, 64, 128, 128
ATT_GROUP = ATT_HEADS // ATT_KV_HEADS
CONV_WIDTH = 512
GLA_HEADS, GLA_KEY_DIM, GLA_VALUE_DIM, GLA_GATE_RANK, GLA_TAU = 4, 256, 512, 16, 16.0
GLA_DK = GLA_KEY_DIM // GLA_HEADS
GLA_DV = GLA_VALUE_DIM // GLA_HEADS
N_EXPERTS, EXPERT_FF, CAPACITY_FACTOR = 16, 2048, 2
DN_ALPHA = (2 * DEPTH) ** 0.25
LN_EPS = 1e-5
RMS_EPS = 1e-6
ATT_Q, ATT_KV = ATT_HEADS * ATT_HEAD_DIM, ATT_KV_HEADS * ATT_HEAD_DIM

LANES = 128
SUBLANES = 8
NEG_BIG = -1e30

GLA_CHUNK = 128
VMEM_LIMIT = 56 * 1024 * 1024


def _cparams(sem, vmem=VMEM_LIMIT):
    return pltpu.CompilerParams(dimension_semantics=sem, vmem_limit_bytes=vmem)


def _proj_body(n_out, x_ref, *refs):
    xb = x_ref[...].astype(BF16)
    for w_ref, o_ref in zip(refs[:n_out], refs[n_out:]):
        o_ref[...] = jnp.dot(xb, w_ref[...], preferred_element_type=F32).astype(o_ref.dtype)


def _proj(x, ws, out_dtypes, tm, name):
    S, D = x.shape
    in_specs = [pl.BlockSpec((tm, D), lambda i: (i, 0))]
    in_specs += [pl.BlockSpec(w.shape, lambda i: (0, 0)) for w in ws]
    out_specs = [pl.BlockSpec((tm, w.shape[1]), lambda i: (i, 0)) for w in ws]
    out_shape = [jax.ShapeDtypeStruct((S, w.shape[1]), dt) for w, dt in zip(ws, out_dtypes)]
    return pl.pallas_call(
        functools.partial(_proj_body, len(ws)), grid=(S // tm,), in_specs=in_specs,
        out_specs=out_specs, out_shape=out_shape, name=name,
        compiler_params=_cparams(("parallel",)))(x, *ws)


ATT_TQ = 4 * ATT_BLOCK


def _attn_bias_table():
    q_off = np.arange(ATT_BLOCK)[:, None]
    k_off = np.arange(3 * ATT_BLOCK)[None, :] - ATT_BLOCK
    dist = np.abs(q_off - k_off).astype(np.float32)
    slopes = 2.0 ** (-8.0 * np.arange(1, ATT_HEADS + 1, dtype=np.float32) / ATT_HEADS)
    tab = np.empty((ATT_KV_HEADS, ATT_GROUP * ATT_BLOCK, 3 * ATT_BLOCK), np.float32)
    for j in range(ATT_KV_HEADS):
        for g in range(ATT_GROUP):
            b = np.where(dist <= ATT_WINDOW, -slopes[j * ATT_GROUP + g] * dist, NEG_BIG)
            tab[j, g * ATT_BLOCK:(g + 1) * ATT_BLOCK] = b
    return tab


def _attn_body(n_blocks, cur_ref, prev_ref, next_ref, bias_ref, sink_ref, o_ref):
    i = pl.program_id(0)
    hd = ATT_HEAD_DIM
    kv_all = jnp.concatenate(
        [prev_ref[:, ATT_Q:], cur_ref[:, ATT_Q:], next_ref[:, ATT_Q:]], axis=0)
    col = lax.broadcasted_iota(I32, (1, 3 * ATT_BLOCK), 1)
    for b in range(ATT_TQ // ATT_BLOCK):
        blk = i * (ATT_TQ // ATT_BLOCK) + b
        kvb = kv_all[b * ATT_BLOCK:(b + 3) * ATT_BLOCK]
        off_seq = jnp.logical_or(jnp.logical_and(col < ATT_BLOCK, blk == 0),
                                 jnp.logical_and(col >= 2 * ATT_BLOCK, blk == n_blocks - 1))
        edge = jnp.where(off_seq, NEG_BIG, 0.0).astype(F32)
        qb = cur_ref[b * ATT_BLOCK:(b + 1) * ATT_BLOCK, :ATT_Q]
        for j in range(ATT_KV_HEADS):
            q4 = jnp.concatenate(
                [qb[:, (j * ATT_GROUP + g) * hd:(j * ATT_GROUP + g + 1) * hd] for g in range(ATT_GROUP)],
                axis=0)
            kj = kvb[:, j * hd:(j + 1) * hd]
            vj = kvb[:, ATT_KV + j * hd:ATT_KV + (j + 1) * hd]
            s = lax.dot_general(q4, kj, (((1,), (1,)), ((), ())), preferred_element_type=F32)
            s = s * (hd ** -0.5) + bias_ref[j] + edge
            sink = sink_ref[j]
            m = jnp.maximum(jnp.max(s, axis=-1, keepdims=True), sink)
            p = jnp.exp(s - m)
            denom = jnp.sum(p, axis=-1, keepdims=True) + jnp.exp(sink - m)
            p = p / denom
            o = jnp.dot(p.astype(BF16), vj, preferred_element_type=F32)
            for g in range(ATT_GROUP):
                h = j * ATT_GROUP + g
                o_ref[b * ATT_BLOCK:(b + 1) * ATT_BLOCK, h * hd:(h + 1) * hd] = (
                    o[g * ATT_BLOCK:(g + 1) * ATT_BLOCK].astype(o_ref.dtype))


def _attention(qkv, sink):
    S = qkv.shape[0]
    n_blocks = S // ATT_BLOCK
    per = ATT_TQ // ATT_BLOCK
    width = qkv.shape[1]
    bias = jnp.asarray(_attn_bias_table())
    sink_col = jnp.repeat(sink.astype(F32).reshape(ATT_KV_HEADS, ATT_GROUP), ATT_BLOCK, axis=1)[..., None]
    return pl.pallas_call(
        functools.partial(_attn_body, n_blocks),
        grid=(S // ATT_TQ,),
        in_specs=[
            pl.BlockSpec((ATT_TQ, width), lambda i: (i, 0)),
            pl.BlockSpec((ATT_BLOCK, width), lambda i: (jnp.maximum(i * per - 1, 0), 0)),
            pl.BlockSpec((ATT_BLOCK, width), lambda i: (jnp.minimum((i + 1) * per, n_blocks - 1), 0)),
            pl.BlockSpec(bias.shape, lambda i: (0, 0, 0)),
            pl.BlockSpec(sink_col.shape, lambda i: (0, 0, 0)),
        ],
        out_specs=pl.BlockSpec((ATT_TQ, ATT_Q), lambda i: (i, 0)),
        out_shape=jax.ShapeDtypeStruct((S, ATT_Q), BF16), name="attn",
        compiler_params=_cparams(("parallel",)))(qkv, qkv, qkv, bias, sink_col)


GLA_TB = 512


def _mid_broadcast(x, w, pos):
    C, N = x.shape
    if 2 * w >= SUBLANES:
        x3 = x.reshape(C // (2 * w), 2 * w, N)
        mid = x3[:, w - 1:w, :]
        return jnp.broadcast_to(mid, x3.shape).reshape(C, N)
    out = jnp.zeros_like(x)
    for o in range(-(w - 1), w + 1):
        src = x if o == 0 else pltpu.roll(x, o % C, 0)
        out = jnp.where(pos - (w - 1) == o, src, out)
    return out


def _gla_body(qk_ref, v_ref, lr_ref, w2_ref, gb_ref, o_ref, state_ref):
    C = GLA_CHUNK
    H, dk, dv = GLA_HEADS, GLA_DK, GLA_DV

    @pl.when(pl.program_id(0) == 0)
    def _():
        state_ref[...] = jnp.zeros_like(state_ref)

    row = lax.broadcasted_iota(I32, (C, 1), 0)
    rowi = lax.broadcasted_iota(I32, (C, C), 0)
    coli = lax.broadcasted_iota(I32, (C, C), 1)
    for c in range(GLA_TB // C):
        rows = slice(c * C, (c + 1) * C)
        q = qk_ref[rows, :GLA_KEY_DIM] * (dk ** -0.5)
        k = qk_ref[rows, GLA_KEY_DIM:]
        v = v_ref[rows, :]
        z = jnp.dot(lr_ref[rows, :].astype(BF16), w2_ref[...], preferred_element_type=F32) + gb_ref[...]
        la = (jnp.minimum(z, 0.0) - jnp.log1p(jnp.exp(-jnp.abs(z)))) * (1.0 / GLA_TAU)

        cw = la
        scores = [jnp.zeros((C, C), F32) for _ in range(H)]
        w = 1
        while w < C:
            pos = row & (2 * w - 1)
            right = pos >= w
            mid = _mid_broadcast(cw, w, pos)
            qt = jnp.where(right, q * jnp.exp(cw), 0.0).astype(BF16)
            kt = jnp.where(right, 0.0, k * jnp.exp(mid - cw)).astype(BF16)
            same = (rowi // (2 * w)) == (coli // (2 * w))
            for h in range(H):
                sc = lax.dot_general(qt[:, h * dk:(h + 1) * dk], kt[:, h * dk:(h + 1) * dk],
                                     (((1,), (1,)), ((), ())), preferred_element_type=F32)
                scores[h] = scores[h] + jnp.where(same, sc, 0.0)
            cw = cw + jnp.where(right, mid, 0.0)
            w *= 2
        b = cw
        btot = b[C - 1:C, :]
        qhat = (q * jnp.exp(b)).astype(BF16)
        khat = (k * jnp.exp(btot - b)).astype(BF16)
        qk_diag = q * k
        for h in range(H):
            ksl = slice(h * dk, (h + 1) * dk)
            vsl = slice(h * dv, (h + 1) * dv)
            diag = jnp.sum(qk_diag[:, ksl], axis=-1, keepdims=True)
            a_mat = scores[h] + jnp.where(rowi == coli, diag, 0.0)
            vh = v[:, vsl]
            st = state_ref[h]
            o_h = jnp.dot(a_mat.astype(BF16), vh.astype(BF16), preferred_element_type=F32)
            o_h = o_h + lax.dot_general(qhat[:, ksl], st.astype(BF16), (((1,), (1,)), ((), ())),
                                        preferred_element_type=F32)
            o_ref[rows, vsl] = o_h
            upd = jnp.dot(vh.T.astype(BF16), khat[:, ksl], preferred_element_type=F32)
            state_ref[h] = st * jnp.exp(btot[:, ksl]) + upd


def _gla_direction(qk, v, lr, w2p, gb):
    S = qk.shape[0]
    return pl.pallas_call(
        _gla_body, grid=(S // GLA_TB,),
        in_specs=[
            pl.BlockSpec((GLA_TB, 2 * GLA_KEY_DIM), lambda i: (i, 0)),
            pl.BlockSpec((GLA_TB, GLA_VALUE_DIM), lambda i: (i, 0)),
            pl.BlockSpec((GLA_TB, LANES), lambda i: (i, 0)),
            pl.BlockSpec(w2p.shape, lambda i: (0, 0)),
            pl.BlockSpec(gb.shape, lambda i: (0, 0)),
        ],
        out_specs=pl.BlockSpec((GLA_TB, GLA_VALUE_DIM), lambda i: (i, 0)),
        out_shape=jax.ShapeDtypeStruct((S, GLA_VALUE_DIM), F32),
        scratch_shapes=[pltpu.VMEM((GLA_HEADS, GLA_DV, GLA_DK), F32)], name="gla",
        compiler_params=_cparams(("arbitrary",)))(qk, v, lr, w2p, gb)


MERGE_TM = 256


def _layer_norm(h, g, b):
    mu = jnp.mean(h, axis=-1, keepdims=True)
    d = h - mu
    var = jnp.mean(d * d, axis=-1, keepdims=True)
    return d * lax.rsqrt(var + LN_EPS) * g + b


def _split3(x):
    hi = x.astype(BF16)
    r1 = x - hi.astype(F32)
    mid = r1.astype(BF16)
    lo = (r1 - mid.astype(F32)).astype(BF16)
    return hi, mid, lo


def _merge_body(n_tiles, x_ref, ya_ref, cv_ref, cvp_ref, cvn_ref, of_ref, ob_ref, r_ref, mg_ref,
                wa_ref, wc_ref, wg_ref, wo_ref, cw_ref, ng_ref, lg_ref, lb_ref, rw_ref,
                x1_ref, aff_ref):
    i = pl.program_id(0)
    T = MERGE_TM
    CW = CONV_WIDTH
    u = cv_ref[:, 2 * CW:] * cv_ref[:, :CW]
    u_before = cvp_ref[SUBLANES - 1:SUBLANES, 2 * CW:] * cvp_ref[SUBLANES - 1:SUBLANES, :CW]
    u_after = cvn_ref[0:1, 2 * CW:] * cvn_ref[0:1, :CW]
    u_before = jnp.where(i == 0, 0.0, u_before)
    u_after = jnp.where(i == n_tiles - 1, 0.0, u_after)
    row = lax.broadcasted_iota(I32, (T, 1), 0)
    u_m1 = jnp.where(row == 0, u_before, pltpu.roll(u, 1, 0))
    u_p1 = jnp.where(row == T - 1, u_after, pltpu.roll(u, T - 1, 0))
    y_conv = cv_ref[:, CW:2 * CW] * (cw_ref[0:1, :] * u_m1 + cw_ref[1:2, :] * u + cw_ref[2:3, :] * u_p1)

    o = of_ref[...] + ob_ref[...]
    parts = []
    for h in range(GLA_HEADS):
        oh = o[:, h * GLA_DV:(h + 1) * GLA_DV]
        parts.append(oh * lax.rsqrt(jnp.mean(oh * oh, axis=-1, keepdims=True) + RMS_EPS))
    r = r_ref[...]
    y_gla = jnp.concatenate(parts, axis=-1) * ng_ref[...] * (r * jax.nn.sigmoid(r))

    D = D_MODEL
    merged = jax.nn.sigmoid(mg_ref[:, :D]) * jnp.dot(ya_ref[...], wa_ref[...], preferred_element_type=F32)
    merged += jax.nn.sigmoid(mg_ref[:, D:2 * D]) * jnp.dot(y_conv.astype(BF16), wc_ref[...],
                                                             preferred_element_type=F32)
    merged += jax.nn.sigmoid(mg_ref[:, 2 * D:]) * jnp.dot(y_gla.astype(BF16), wg_ref[...],
                                                           preferred_element_type=F32)
    hres = DN_ALPHA * x_ref[...] + jnp.dot(merged.astype(BF16), wo_ref[...], preferred_element_type=F32)
    x1 = _layer_norm(hres, lg_ref[...], lb_ref[...])
    x1_ref[...] = x1

    nt = (((1,), (1,)), ((), ()))
    xh, xm, xl = _split3(x1)
    wh, wm, wl = rw_ref[0], rw_ref[1], rw_ref[2]
    logits = lax.dot_general(wh, xh, nt, preferred_element_type=F32)
    for a, bb in ((wh, xm), (wm, xh), (wh, xl), (wm, xm), (wl, xh)):
        logits += lax.dot_general(a, bb, nt, preferred_element_type=F32)
    mx = jnp.max(logits, axis=0, keepdims=True)
    ex = jnp.exp(logits - mx)
    aff_ref[...] = ex / jnp.sum(ex, axis=0, keepdims=True)


def _merge(x, y_attn, conv, o_f, o_b, r, mg, wa, wc, wg, wo, cw, ng, lg, lb, rw3):
    S, D = x.shape
    T = MERGE_TM
    n_tiles = S // T
    per = T // SUBLANES
    n8 = S // SUBLANES
    row_spec = lambda wd: pl.BlockSpec((T, wd), lambda i: (i, 0))
    full = lambda a: pl.BlockSpec(a.shape, lambda i: (0,) * a.ndim)
    return pl.pallas_call(
        functools.partial(_merge_body, n_tiles), grid=(n_tiles,),
        in_specs=[
            row_spec(D), row_spec(ATT_Q), row_spec(3 * CONV_WIDTH),
            pl.BlockSpec((SUBLANES, 3 * CONV_WIDTH), lambda i: (jnp.maximum(i * per - 1, 0), 0)),
            pl.BlockSpec((SUBLANES, 3 * CONV_WIDTH), lambda i: (jnp.minimum((i + 1) * per, n8 - 1), 0)),
            row_spec(GLA_VALUE_DIM), row_spec(GLA_VALUE_DIM), row_spec(GLA_VALUE_DIM), row_spec(3 * D),
            full(wa), full(wc), full(wg), full(wo), full(cw), full(ng), full(lg), full(lb), full(rw3),
        ],
        out_specs=[row_spec(D), pl.BlockSpec((N_EXPERTS, T), lambda i: (0, i))],
        out_shape=[jax.ShapeDtypeStruct((S, D), F32), jax.ShapeDtypeStruct((N_EXPERTS, S), F32)],
        name="merge", compiler_params=_cparams(("parallel",)))(
            x, y_attn, conv, conv, conv, o_f, o_b, r, mg, wa, wc, wg, wo, cw, ng, lg, lb, rw3)


def _thr_body(k, aff_ref, thr_ref):
    a = aff_ref[...]

    def step(t, thr):
        cand = thr | jnp.left_shift(jnp.int32(1), 30 - t)
        ge = (a >= pltpu.bitcast(cand, F32)).astype(I32)
        cnt = jnp.sum(jnp.sum(ge, axis=1, keepdims=True), axis=2, keepdims=True)
        return jnp.where(cnt >= k, cand, thr)

    thr = lax.fori_loop(0, 31, step, jnp.zeros((N_EXPERTS, 1, 1), I32))
    thr_ref[...] = jnp.broadcast_to(thr, thr_ref.shape)


def _select_body(k, aff_ref, thr_ref, idx_ref, gate_ref, offs_ref):
    R = aff_ref.shape[0]
    L = LANES
    a = aff_ref[...]
    thr_bits = thr_ref[...]
    thr = pltpu.bitcast(thr_bits, F32)[0:1, 0:1]
    thr_next = pltpu.bitcast(thr_bits + 1, F32)[0:1, 0:1]
    gt = a >= thr_next
    eq = jnp.logical_and(a >= thr, jnp.logical_not(gt))
    n_gt = jnp.sum(jnp.sum(gt.astype(F32), axis=0, keepdims=True), axis=1, keepdims=True)
    need = float(k) - n_gt

    li = lax.broadcasted_iota(I32, (L, L), 0)
    lj = lax.broadcasted_iota(I32, (L, L), 1)
    upper = jnp.where(li <= lj, 1.0, 0.0).astype(BF16)
    ri = lax.broadcasted_iota(I32, (R, R), 0)
    rj = lax.broadcasted_iota(I32, (R, R), 1)
    lower_strict = jnp.where(rj < ri, 1.0, 0.0).astype(BF16)

    def prefix(mask_f):
        cum = jnp.dot(mask_f.astype(BF16), upper, preferred_element_type=F32)
        tot = jnp.broadcast_to(cum[:, L - 1:L], (R, L))
        offs = jnp.dot(lower_strict, tot.astype(BF16), preferred_element_type=F32)
        return cum, tot, offs

    eq_f = eq.astype(F32)
    cum_eq, _, offs_eq = prefix(eq_f)
    rank_eq = cum_eq - eq_f + offs_eq
    sel = jnp.logical_or(gt, jnp.logical_and(eq, rank_eq < need))
    sel_f = sel.astype(F32)
    cum, tot, offs = prefix(sel_f)
    offs_ref[...] = offs.astype(I32)

    offs_incl = offs + tot
    cum_t = cum.T.astype(BF16)
    a_t = a.T
    ah, am, al = _split3(a_t)
    row_iota = lax.broadcasted_iota(I32, (R, L), 0).astype(F32)
    lane_iota_col = lax.broadcasted_iota(I32, (L, L), 0).astype(F32)
    slot_lane = lax.broadcasted_iota(I32, (1, L), 1).astype(F32)

    def chunk(ci, carry):
        slot = slot_lane + lax.convert_element_type(ci * L, F32)
        row_of = jnp.sum((offs_incl <= slot).astype(F32), axis=0, keepdims=True)
        onehot = (row_iota == row_of).astype(F32)
        base = jnp.sum(onehot * offs, axis=0, keepdims=True)
        local = slot - base
        oh = onehot.astype(BF16)
        cum_at = jnp.dot(cum_t, oh, preferred_element_type=F32)
        lane_of = jnp.sum((cum_at <= local).astype(F32), axis=0, keepdims=True)
        val = (jnp.dot(ah, oh, preferred_element_type=F32) + jnp.dot(am, oh, preferred_element_type=F32)
               + jnp.dot(al, oh, preferred_element_type=F32))
        gate = jnp.sum(jnp.where(lane_iota_col == lane_of, val, 0.0), axis=0, keepdims=True)
        idx_ref[pl.ds(ci, 1), :] = (row_of * L + lane_of).astype(I32)
        gate_ref[pl.ds(ci, 1), :] = gate
        return carry

    lax.fori_loop(0, idx_ref.shape[0], chunk, 0)


def _topk(aff_t, k):
    E, S = aff_t.shape
    R = S // LANES
    aff3 = aff_t.reshape(E, R, LANES)
    thr = pl.pallas_call(
        functools.partial(_thr_body, k), grid=(1,),
        in_specs=[pl.BlockSpec((E, R, LANES), lambda i: (0, 0, 0))],
        out_specs=pl.BlockSpec((E, SUBLANES, LANES), lambda i: (0, 0, 0)),
        out_shape=jax.ShapeDtypeStruct((E, SUBLANES, LANES), I32), name="topk_thr",
        compiler_params=_cparams(("arbitrary",)))(aff3)
    n_chunks = k // LANES
    idx, gate, offs = pl.pallas_call(
        functools.partial(_select_body, k), grid=(E,),
        in_specs=[pl.BlockSpec((None, R, LANES), lambda e: (e, 0, 0)),
                  pl.BlockSpec((None, SUBLANES, LANES), lambda e: (e, 0, 0))],
        out_specs=[pl.BlockSpec((None, n_chunks, LANES), lambda e: (e, 0, 0)),
                   pl.BlockSpec((None, n_chunks, LANES), lambda e: (e, 0, 0)),
                   pl.BlockSpec((None, R, LANES), lambda e: (e, 0, 0))],
        out_shape=[jax.ShapeDtypeStruct((E, n_chunks, LANES), I32),
                   jax.ShapeDtypeStruct((E, n_chunks, LANES), F32),
                   jax.ShapeDtypeStruct((E, R, LANES), I32)],
        name="topk_select", compiler_params=_cparams(("parallel",)))(aff3, thr)
    return idx.reshape(E, k), gate.reshape(E, k), offs[:, :, 0]


GATHER_ROWS = 1024


def _gather_body(idx_ref, x_ref, o_ref, sem):
    base = pl.program_id(0) * GATHER_ROWS

    def copy(j):
        return pltpu.make_async_copy(x_ref.at[idx_ref[0, j]], o_ref.at[base + j], sem)

    def issue(j, c):
        copy(j).start()
        return c

    def drain(j, c):
        copy(j).wait()
        return c

    lax.fori_loop(0, GATHER_ROWS, issue, 0)
    lax.fori_loop(0, GATHER_ROWS, drain, 0)


def _gather_rows(x3, idx_flat):
    N = idx_flat.shape[0]
    steps = N // GATHER_ROWS
    idx2 = idx_flat.reshape(steps, 1, GATHER_ROWS)
    return pl.pallas_call(
        _gather_body, grid=(steps,),
        in_specs=[pl.BlockSpec((None, 1, GATHER_ROWS), lambda i: (i, 0, 0), memory_space=pltpu.SMEM),
                  pl.BlockSpec(memory_space=pl.ANY)],
        out_specs=pl.BlockSpec(memory_space=pl.ANY),
        out_shape=jax.ShapeDtypeStruct((N,) + x3.shape[1:], x3.dtype),
        scratch_shapes=[pltpu.SemaphoreType.DMA(())], name="gather",
        compiler_params=_cparams(("arbitrary",)))(idx2, x3)


FFN_TM = 1024
FFN_TF = 512


def _ffn_body(xs_ref, gate_ref, wg_ref, wu_ref, wd_ref, y_ref, xb_ref):
    f = pl.program_id(2)

    @pl.when(f == 0)
    def _():
        xb_ref[...] = xs_ref[...].astype(BF16)

    xb = xb_ref[...]
    hg = jnp.dot(xb, wg_ref[...].astype(BF16), preferred_element_type=F32)
    hu = jnp.dot(xb, wu_ref[...].astype(BF16), preferred_element_type=F32)
    hh = (hg * jax.nn.sigmoid(hg) * hu).astype(BF16)
    part = jnp.dot(hh, wd_ref[...].astype(BF16), preferred_element_type=F32)

    @pl.when(f == 0)
    def _():
        y_ref[...] = part

    @pl.when(f > 0)
    def _():
        y_ref[...] += part

    @pl.when(f == pl.num_programs(2) - 1)
    def _():
        y_ref[...] = y_ref[...] * gate_ref[...]


def _expert_ffn(xs, gate_col, w_gate, w_up, w_down):
    E, cap, D = xs.shape
    FF = w_gate.shape[-1]
    tm = min(FFN_TM, cap)
    return pl.pallas_call(
        _ffn_body, grid=(E, cap // tm, FF // FFN_TF),
        in_specs=[
            pl.BlockSpec((None, tm, D), lambda e, m, f: (e, m, 0)),
            pl.BlockSpec((None, tm, 1), lambda e, m, f: (e, m, 0)),
            pl.BlockSpec((None, D, FFN_TF), lambda e, m, f: (e, 0, f)),
            pl.BlockSpec((None, D, FFN_TF), lambda e, m, f: (e, 0, f)),
            pl.BlockSpec((None, FFN_TF, D), lambda e, m, f: (e, f, 0)),
        ],
        out_specs=pl.BlockSpec((None, tm, D), lambda e, m, f: (e, m, 0)),
        out_shape=jax.ShapeDtypeStruct((E, cap, D), F32),
        scratch_shapes=[pltpu.VMEM((tm, D), BF16)], name="ffn",
        compiler_params=_cparams(("parallel", "parallel", "arbitrary")))(xs, gate_col, w_gate, w_up, w_down)


COMBINE_PASSES = 4
COMBINE_ROWS = 256


def _combine_body(cap, q_tokens, n_rows_total, start_ref, idx_ref, y_ref, acc_ref, buf_ref, sem):
    qi = pl.program_id(0)
    e = pl.program_id(1)

    @pl.when(e == 0)
    def _():
        acc_ref[...] = jnp.zeros_like(acc_ref)

    a = start_ref[e, qi]
    b = start_ref[e, qi + 1]
    n_chunks = (b - a + COMBINE_ROWS - 1) // COMBINE_ROWS
    tok0 = qi * q_tokens

    def chunk_src(ci):
        r0 = jnp.minimum(e * cap + a + ci * COMBINE_ROWS, n_rows_total - COMBINE_ROWS)
        return r0

    def copy(ci, slot):
        return pltpu.make_async_copy(y_ref.at[pl.ds(chunk_src(ci), COMBINE_ROWS)], buf_ref.at[slot], sem.at[slot])

    @pl.when(n_chunks > 0)
    def _():
        copy(0, 0).start()

    def chunk(ci, carry):
        slot = ci % 2

        @pl.when(ci + 1 < n_chunks)
        def _():
            copy(ci + 1, 1 - slot).start()

        copy(ci, slot).wait()
        lo = a + ci * COMBINE_ROWS
        hi = jnp.minimum(lo + COMBINE_ROWS, b)
        shift = e * cap - chunk_src(ci)

        def add(j, c):
            t = idx_ref[0, j] - tok0
            acc_ref[t] += buf_ref[slot, j + shift]
            return c

        lax.fori_loop(lo, hi, add, 0)
        return carry

    lax.fori_loop(0, n_chunks, chunk, 0)


def _combine(y_rows, idx, starts, S):
    E, cap = idx.shape
    q_tokens = S // COMBINE_PASSES
    n_rows_total = y_rows.shape[0]
    grid_spec = pltpu.PrefetchScalarGridSpec(
        num_scalar_prefetch=1, grid=(COMBINE_PASSES, E),
        in_specs=[pl.BlockSpec((None, 1, cap), lambda q, e, st: (e, 0, 0), memory_space=pltpu.SMEM),
                  pl.BlockSpec(memory_space=pl.ANY)],
        out_specs=pl.BlockSpec((q_tokens, SUBLANES, LANES), lambda q, e, st: (q, 0, 0)),
        scratch_shapes=[pltpu.VMEM((2, COMBINE_ROWS, SUBLANES, LANES), F32), pltpu.SemaphoreType.DMA((2,))])
    return pl.pallas_call(
        functools.partial(_combine_body, cap, q_tokens, n_rows_total), grid_spec=grid_spec,
        out_shape=jax.ShapeDtypeStruct((S, SUBLANES, LANES), F32), name="combine",
        compiler_params=_cparams(("arbitrary", "arbitrary")))(starts, idx.reshape(E, 1, cap), y_rows)


LN_TM = 512


def _ln_body(x_ref, f_ref, g_ref, b_ref, o_ref):
    o_ref[...] = _layer_norm(DN_ALPHA * x_ref[...] + f_ref[...], g_ref[...], b_ref[...])


def _residual_ln(x, f, g, b):
    S, D = x.shape
    row = pl.BlockSpec((LN_TM, D), lambda i: (i, 0))
    vec = pl.BlockSpec((1, D), lambda i: (0, 0))
    return pl.pallas_call(
        _ln_body, grid=(S // LN_TM,), in_specs=[row, row, vec, vec], out_specs=row,
        out_shape=jax.ShapeDtypeStruct((S, D), F32), name="ln",
        compiler_params=_cparams(("parallel",)))(x, f, g, b)


PROJ_TM = 512


def _pad_cols(w, n):
    return jnp.pad(w, ((0, 0), (0, n - w.shape[1])))


def _layer(x, p):
    S, D = x.shape
    w_in = p["w_in"].astype(BF16)
    c0 = ATT_Q + 2 * ATT_KV
    c1 = c0 + 3 * CONV_WIDTH
    c2 = c1 + 2 * GLA_KEY_DIM
    c3 = c2 + GLA_VALUE_DIM
    c4 = c3 + GLA_VALUE_DIM
    c5 = c4 + 2 * GLA_GATE_RANK
    ws = [w_in[:, :c0], w_in[:, c0:c1], w_in[:, c1:c2], w_in[:, c2:c3], w_in[:, c3:c4],
          _pad_cols(w_in[:, c4:c5], LANES)]
    qkv, conv, gqk, gv, gr, glr = _proj(x, ws, [BF16, F32, F32, F32, F32, F32], PROJ_TM, "proj_mix")
    (mg,) = _proj(x, [w_in[:, c5:]], [F32], PROJ_TM, "proj_gates")

    y_attn = _attention(qkv, p["attn_sink"])

    w2 = p["gla_gate_w2"].astype(BF16)
    rk = GLA_GATE_RANK
    w2_f = jnp.zeros((LANES, GLA_KEY_DIM), BF16).at[:rk].set(w2[0])
    w2_b = jnp.zeros((LANES, GLA_KEY_DIM), BF16).at[rk:2 * rk].set(w2[1])
    gb = p["gla_gate_b"].astype(F32)
    o_f = _gla_direction(gqk, gv, glr, w2_f, gb[0:1])
    flip = lambda t: jnp.flip(t, axis=0)
    o_b = flip(_gla_direction(flip(gqk), flip(gv), flip(glr), w2_b, gb[1:2]))

    rw3 = jnp.stack(_split3(p["router_w"].T.astype(F32)))
    x1, aff_t = _merge(
        x, y_attn, conv, o_f, o_b, gr, mg,
        p["w_branch_attn"].astype(BF16), p["w_branch_conv"].astype(BF16), p["w_branch_gla"].astype(BF16),
        p["w_out"].astype(BF16), p["conv_w"].astype(F32), p["gla_norm_g"].astype(F32).reshape(1, -1),
        p["ln_mix_g"].astype(F32).reshape(1, -1), p["ln_mix_b"].astype(F32).reshape(1, -1), rw3)

    cap = CAPACITY_FACTOR * S // N_EXPERTS
    idx, gate, offs = _topk(aff_t, cap)
    xs = _gather_rows(x1.reshape(S, SUBLANES, LANES), idx.reshape(-1))
    y = _expert_ffn(xs.reshape(N_EXPERTS, cap, D), gate.reshape(N_EXPERTS, cap, 1),
                    p["expert_w_gate"], p["expert_w_up"], p["expert_w_down"])
    rows_per_pass = S // COMBINE_PASSES // LANES
    starts = jnp.concatenate([offs[:, ::rows_per_pass], jnp.full((N_EXPERTS, 1), cap, I32)], axis=1)
    ffn = _combine(y.reshape(N_EXPERTS * cap, SUBLANES, LANES), idx, starts, S).reshape(S, D)
    return _residual_ln(x1, ffn, p["ln_ffn_g"].astype(F32).reshape(1, -1), p["ln_ffn_b"].astype(F32).reshape(1, -1))


def kernel(x, w_in, attn_sink, conv_w, gla_gate_w2, gla_gate_b, gla_norm_g, w_branch_attn, w_branch_conv,
           w_branch_gla, w_out, ln_mix_g, ln_mix_b, router_w, expert_w_gate, expert_w_up, expert_w_down,
           ln_ffn_g, ln_ffn_b):
    params = dict(w_in=w_in, attn_sink=attn_sink, conv_w=conv_w, gla_gate_w2=gla_gate_w2, gla_gate_b=gla_gate_b,
                  gla_norm_g=gla_norm_g, w_branch_attn=w_branch_attn, w_branch_conv=w_branch_conv,
                  w_branch_gla=w_branch_gla, w_out=w_out, ln_mix_g=ln_mix_g, ln_mix_b=ln_mix_b,
                  router_w=router_w, expert_w_gate=expert_w_gate, expert_w_up=expert_w_up,
                  expert_w_down=expert_w_down, ln_ffn_g=ln_ffn_g, ln_ffn_b=ln_ffn_b)
    B, S, D = x.shape
    outs = []
    for bi in range(B):
        h = x[bi]
        for l in range(DEPTH):
            h = _layer(h, {name: val[l] for name, val in params.items()})
        outs.append(h)
    return jnp.stack(outs)
```

```python
import functools
import math

import numpy as np
import jax
import jax.numpy as jnp
from jax import lax
from jax.experimental import pallas as pl
from jax.experimental.pallas import tpu as pltpu

F32 = jnp.float32
BF16 = jnp.bfloat16
I32 = jnp.int32

D_MODEL = 1024
DEPTH = 2
ATT_HEADS, ATT_KV_HEADS, ATT_HEAD_DIM, ATT_BLOCK, ATT_WINDOW = 8, 2, 64, 128, 128
ATT_GROUP = ATT_HEADS // ATT_KV_HEADS
CONV_WIDTH = 512
GLA_HEADS, GLA_KEY_DIM, GLA_VALUE_DIM, GLA_GATE_RANK, GLA_TAU = 4, 256, 512, 16, 16.0
GLA_DK = GLA_KEY_DIM // GLA_HEADS
GLA_DV = GLA_VALUE_DIM // GLA_HEADS
N_EXPERTS, EXPERT_FF, CAPACITY_FACTOR = 16, 2048, 2
DN_ALPHA = (2 * DEPTH) ** 0.25
LN_EPS = 1e-5
RMS_EPS = 1e-6
ATT_Q, ATT_KV = ATT_HEADS * ATT_HEAD_DIM, ATT_KV_HEADS * ATT_HEAD_DIM

LANES = 128
SUBLANES = 8
NEG_BIG = -1e30

GLA_CHUNK = 128
VMEM_LIMIT = 56 * 1024 * 1024


def _cparams(sem, vmem=VMEM_LIMIT):
    return pltpu.CompilerParams(dimension_semantics=sem, vmem_limit_bytes=vmem)


def _proj_body(n_out, x_ref, *refs):
    xb = x_ref[...].astype(BF16)
    for w_ref, o_ref in zip(refs[:n_out], refs[n_out:]):
        o_ref[...] = jnp.dot(xb, w_ref[...], preferred_element_type=F32).astype(o_ref.dtype)


def _proj(x, ws, out_dtypes, tm, name):
    S, D = x.shape
    in_specs = [pl.BlockSpec((tm, D), lambda i: (i, 0))]
    in_specs += [pl.BlockSpec(w.shape, lambda i: (0, 0)) for w in ws]
    out_specs = [pl.BlockSpec((tm, w.shape[1]), lambda i: (i, 0)) for w in ws]
    out_shape = [jax.ShapeDtypeStruct((S, w.shape[1]), dt) for w, dt in zip(ws, out_dtypes)]
    return pl.pallas_call(
        functools.partial(_proj_body, len(ws)), grid=(S // tm,), in_specs=in_specs,
        out_specs=out_specs, out_shape=out_shape, name=name,
        compiler_params=_cparams(("parallel",)))(x, *ws)


ATT_TQ = 4 * ATT_BLOCK


def _attn_bias_table():
    q_off = np.arange(ATT_BLOCK)[:, None]
    k_off = np.arange(3 * ATT_BLOCK)[None, :] - ATT_BLOCK
    dist = np.abs(q_off - k_off).astype(np.float32)
    slopes = 2.0 ** (-8.0 * np.arange(1, ATT_HEADS + 1, dtype=np.float32) / ATT_HEADS)
    tab = np.empty((ATT_KV_HEADS, ATT_GROUP * ATT_BLOCK, 3 * ATT_BLOCK), np.float32)
    for j in range(ATT_KV_HEADS):
        for g in range(ATT_GROUP):
            b = np.where(dist <= ATT_WINDOW, -slopes[j * ATT_GROUP + g] * dist, NEG_BIG)
            tab[j, g * ATT_BLOCK:(g + 1) * ATT_BLOCK] = b
    return tab


def _attn_body(n_blocks, cur_ref, prev_ref, next_ref, bias_ref, sink_ref, o_ref):
    i = pl.program_id(0)
    hd = ATT_HEAD_DIM
    kv_all = jnp.concatenate(
        [prev_ref[:, ATT_Q:], cur_ref[:, ATT_Q:], next_ref[:, ATT_Q:]], axis=0)
    col = lax.broadcasted_iota(I32, (1, 3 * ATT_BLOCK), 1)
    for b in range(ATT_TQ // ATT_BLOCK):
        blk = i * (ATT_TQ // ATT_BLOCK) + b
        kvb = kv_all[b * ATT_BLOCK:(b + 3) * ATT_BLOCK]
        off_seq = jnp.logical_or(jnp.logical_and(col < ATT_BLOCK, blk == 0),
                                 jnp.logical_and(col >= 2 * ATT_BLOCK, blk == n_blocks - 1))
        edge = jnp.where(off_seq, NEG_BIG, 0.0).astype(F32)
        qb = cur_ref[b * ATT_BLOCK:(b + 1) * ATT_BLOCK, :ATT_Q]
        for j in range(ATT_KV_HEADS):
            q4 = jnp.concatenate(
                [qb[:, (j * ATT_GROUP + g) * hd:(j * ATT_GROUP + g + 1) * hd] for g in range(ATT_GROUP)],
                axis=0)
            kj = kvb[:, j * hd:(j + 1) * hd]
            vj = kvb[:, ATT_KV + j * hd:ATT_KV + (j + 1) * hd]
            s = lax.dot_general(q4, kj, (((1,), (1,)), ((), ())), preferred_element_type=F32)
            s = s * (hd ** -0.5) + bias_ref[j] + edge
            sink = sink_ref[j]
            m = jnp.maximum(jnp.max(s, axis=-1, keepdims=True), sink)
            p = jnp.exp(s - m)
            denom = jnp.sum(p, axis=-1, keepdims=True) + jnp.exp(sink - m)
            p = p / denom
            o = jnp.dot(p.astype(BF16), vj, preferred_element_type=F32)
            for g in range(ATT_GROUP):
                h = j * ATT_GROUP + g
                o_ref[b * ATT_BLOCK:(b + 1) * ATT_BLOCK, h * hd:(h + 1) * hd] = (
                    o[g * ATT_BLOCK:(g + 1) * ATT_BLOCK].astype(o_ref.dtype))


def _attention(qkv, sink):
    S = qkv.shape[0]
    n_blocks = S // ATT_BLOCK
    per = ATT_TQ // ATT_BLOCK
    width = qkv.shape[1]
    bias = jnp.asarray(_attn_bias_table())
    sink_col = jnp.repeat(sink.astype(F32).reshape(ATT_KV_HEADS, ATT_GROUP), ATT_BLOCK, axis=1)[..., None]
    return pl.pallas_call(
        functools.partial(_attn_body, n_blocks),
        grid=(S // ATT_TQ,),
        in_specs=[
            pl.BlockSpec((ATT_TQ, width), lambda i: (i, 0)),
            pl.BlockSpec((ATT_BLOCK, width), lambda i: (jnp.maximum(i * per - 1, 0), 0)),
            pl.BlockSpec((ATT_BLOCK, width), lambda i: (jnp.minimum((i + 1) * per, n_blocks - 1), 0)),
            pl.BlockSpec(bias.shape, lambda i: (0, 0, 0)),
            pl.BlockSpec(sink_col.shape, lambda i: (0, 0, 0)),
        ],
        out_specs=pl.BlockSpec((ATT_TQ, ATT_Q), lambda i: (i, 0)),
        out_shape=jax.ShapeDtypeStruct((S, ATT_Q), BF16), name="attn",
        compiler_params=_cparams(("parallel",)))(qkv, qkv, qkv, bias, sink_col)


GLA_TB = 512


def _mid_broadcast(x, w, pos, reverse):
    C, N = x.shape
    if 2 * w >= SUBLANES:
        x3 = x.reshape(C // (2 * w), 2 * w, N)
        r = w if reverse else w - 1
        return jnp.broadcast_to(x3[:, r:r + 1, :], x3.shape).reshape(C, N)
    out = jnp.zeros_like(x)
    for o in range(-(w - 1), w + 1):
        shift = (-o if reverse else o) % C
        src = x if shift == 0 else pltpu.roll(x, shift, 0)
        out = jnp.where(pos - (w - 1) == o, src, out)
    return out


def _gla_body(reverse, qk_ref, v_ref, lr_ref, w2_ref, gb_ref, o_ref, state_ref):
    C = GLA_CHUNK
    H, dk, dv = GLA_HEADS, GLA_DK, GLA_DV

    @pl.when(pl.program_id(0) == 0)
    def _():
        state_ref[...] = jnp.zeros_like(state_ref)

    row = lax.broadcasted_iota(I32, (C, 1), 0)
    rowi = lax.broadcasted_iota(I32, (C, C), 0)
    coli = lax.broadcasted_iota(I32, (C, C), 1)
    n_chunks = GLA_TB // C
    for c in (range(n_chunks - 1, -1, -1) if reverse else range(n_chunks)):
        rows = slice(c * C, (c + 1) * C)
        q = qk_ref[rows, :GLA_KEY_DIM] * (dk ** -0.5)
        k = qk_ref[rows, GLA_KEY_DIM:]
        v = v_ref[rows, :]
        z = jnp.dot(lr_ref[rows, :].astype(BF16), w2_ref[...], preferred_element_type=F32) + gb_ref[...]
        la = (jnp.minimum(z, 0.0) - jnp.log1p(jnp.exp(-jnp.abs(z)))) * (1.0 / GLA_TAU)

        cw = la
        scores = [jnp.zeros((C, C), F32) for _ in range(H)]
        w = 1
        while w < C:
            pos = row & (2 * w - 1)
            if reverse:
                pos = (2 * w - 1) - pos
            right = pos >= w
            mid = _mid_broadcast(cw, w, pos, reverse)
            qt = jnp.where(right, q * jnp.exp(cw), 0.0).astype(BF16)
            kt = jnp.where(right, 0.0, k * jnp.exp(mid - cw)).astype(BF16)
            shift = int(math.log2(2 * w))
            same = jnp.right_shift(rowi, shift) == jnp.right_shift(coli, shift)
            for h in range(H):
                sc = lax.dot_general(qt[:, h * dk:(h + 1) * dk], kt[:, h * dk:(h + 1) * dk],
                                     (((1,), (1,)), ((), ())), preferred_element_type=F32)
                scores[h] = scores[h] + jnp.where(same, sc, 0.0)
            cw = cw + jnp.where(right, mid, 0.0)
            w *= 2
        b = cw
        btot = b[0:1, :] if reverse else b[C - 1:C, :]
        qhat = (q * jnp.exp(b)).astype(BF16)
        khat = (k * jnp.exp(btot - b)).astype(BF16)
        qk_diag = q * k
        for h in range(H):
            ksl = slice(h * dk, (h + 1) * dk)
            vsl = slice(h * dv, (h + 1) * dv)
            diag = jnp.sum(qk_diag[:, ksl], axis=-1, keepdims=True)
            a_mat = scores[h] + jnp.where(rowi == coli, diag, 0.0)
            vh = v[:, vsl]
            st = state_ref[h]
            o_h = jnp.dot(a_mat.astype(BF16), vh.astype(BF16), preferred_element_type=F32)
            o_h = o_h + lax.dot_general(qhat[:, ksl], st.astype(BF16), (((1,), (1,)), ((), ())),
                                        preferred_element_type=F32)
            o_ref[rows, vsl] = o_h
            upd = jnp.dot(vh.T.astype(BF16), khat[:, ksl], preferred_element_type=F32)
            state_ref[h] = st * jnp.exp(btot[:, ksl]) + upd


def _gla_direction(qk, v, lr, w2p, gb, reverse):
    S = qk.shape[0]
    n = S // GLA_TB
    blk = (lambda i: (n - 1 - i, 0)) if reverse else (lambda i: (i, 0))
    return pl.pallas_call(
        functools.partial(_gla_body, reverse), grid=(n,),
        in_specs=[
            pl.BlockSpec((GLA_TB, 2 * GLA_KEY_DIM), blk),
            pl.BlockSpec((GLA_TB, GLA_VALUE_DIM), blk),
            pl.BlockSpec((GLA_TB, LANES), blk),
            pl.BlockSpec(w2p.shape, lambda i: (0, 0)),
            pl.BlockSpec(gb.shape, lambda i: (0, 0)),
        ],
        out_specs=pl.BlockSpec((GLA_TB, GLA_VALUE_DIM), blk),
        out_shape=jax.ShapeDtypeStruct((S, GLA_VALUE_DIM), F32),
        scratch_shapes=[pltpu.VMEM((GLA_HEADS, GLA_DV, GLA_DK), F32)], name="gla",
        compiler_params=_cparams(("arbitrary",)))(qk, v, lr, w2p, gb)


MERGE_TM = 256


def _layer_norm(h, g, b):
    mu = jnp.mean(h, axis=-1, keepdims=True)
    d = h - mu
    var = jnp.mean(d * d, axis=-1, keepdims=True)
    return d * lax.rsqrt(var + LN_EPS) * g + b


def _split3(x):
    hi = x.astype(BF16)
    r1 = x - hi.astype(F32)
    mid = r1.astype(BF16)
    lo = (r1 - mid.astype(F32)).astype(BF16)
    return hi, mid, lo


def _merge_body(n_tiles, x_ref, ya_ref, cv_ref, cvp_ref, cvn_ref, of_ref, ob_ref, r_ref, mg_ref,
                wa_ref, wc_ref, wg_ref, wo_ref, cw_ref, ng_ref, lg_ref, lb_ref, rw_ref,
                x1_ref, aff_ref):
    i = pl.program_id(0)
    T = MERGE_TM
    CW = CONV_WIDTH
    u = cv_ref[:, 2 * CW:] * cv_ref[:, :CW]
    u_before = cvp_ref[SUBLANES - 1:SUBLANES, 2 * CW:] * cvp_ref[SUBLANES - 1:SUBLANES, :CW]
    u_after = cvn_ref[0:1, 2 * CW:] * cvn_ref[0:1, :CW]
    u_before = jnp.where(i == 0, 0.0, u_before)
    u_after = jnp.where(i == n_tiles - 1, 0.0, u_after)
    row = lax.broadcasted_iota(I32, (T, 1), 0)
    u_m1 = jnp.where(row == 0, u_before, pltpu.roll(u, 1, 0))
    u_p1 = jnp.where(row == T - 1, u_after, pltpu.roll(u, T - 1, 0))
    y_conv = cv_ref[:, CW:2 * CW] * (cw_ref[0:1, :] * u_m1 + cw_ref[1:2, :] * u + cw_ref[2:3, :] * u_p1)

    o = of_ref[...] + ob_ref[...]
    parts = []
    for h in range(GLA_HEADS):
        oh = o[:, h * GLA_DV:(h + 1) * GLA_DV]
        parts.append(oh * lax.rsqrt(jnp.mean(oh * oh, axis=-1, keepdims=True) + RMS_EPS))
    r = r_ref[...]
    y_gla = jnp.concatenate(parts, axis=-1) * ng_ref[...] * (r * jax.nn.sigmoid(r))

    D = D_MODEL
    merged = jax.nn.sigmoid(mg_ref[:, :D]) * jnp.dot(ya_ref[...], wa_ref[...], preferred_element_type=F32)
    merged += jax.nn.sigmoid(mg_ref[:, D:2 * D]) * jnp.dot(y_conv.astype(BF16), wc_ref[...],
                                                             preferred_element_type=F32)
    merged += jax.nn.sigmoid(mg_ref[:, 2 * D:]) * jnp.dot(y_gla.astype(BF16), wg_ref[...],
                                                           preferred_element_type=F32)
    hres = DN_ALPHA * x_ref[...] + jnp.dot(merged.astype(BF16), wo_ref[...], preferred_element_type=F32)
    x1 = _layer_norm(hres, lg_ref[...], lb_ref[...])
    x1_ref[...] = x1

    nt = (((1,), (1,)), ((), ()))
    xh, xm, xl = _split3(x1)
    wh, wm, wl = rw_ref[0], rw_ref[1], rw_ref[2]
    logits = lax.dot_general(wh, xh, nt, preferred_element_type=F32)
    for a, bb in ((wh, xm), (wm, xh), (wh, xl), (wm, xm), (wl, xh)):
        logits += lax.dot_general(a, bb, nt, preferred_element_type=F32)
    mx = jnp.max(logits, axis=0, keepdims=True)
    ex = jnp.exp(logits - mx)
    aff_ref[...] = ex / jnp.sum(ex, axis=0, keepdims=True)


def _merge(x, y_attn, conv, o_f, o_b, r, mg, wa, wc, wg, wo, cw, ng, lg, lb, rw3):
    S, D = x.shape
    T = MERGE_TM
    n_tiles = S // T
    per = T // SUBLANES
    n8 = S // SUBLANES
    row_spec = lambda wd: pl.BlockSpec((T, wd), lambda i: (i, 0))
    full = lambda a: pl.BlockSpec(a.shape, lambda i: (0,) * a.ndim)
    return pl.pallas_call(
        functools.partial(_merge_body, n_tiles), grid=(n_tiles,),
        in_specs=[
            row_spec(D), row_spec(ATT_Q), row_spec(3 * CONV_WIDTH),
            pl.BlockSpec((SUBLANES, 3 * CONV_WIDTH), lambda i: (jnp.maximum(i * per - 1, 0), 0)),
            pl.BlockSpec((SUBLANES, 3 * CONV_WIDTH), lambda i: (jnp.minimum((i + 1) * per, n8 - 1), 0)),
            row_spec(GLA_VALUE_DIM), row_spec(GLA_VALUE_DIM), row_spec(GLA_VALUE_DIM), row_spec(3 * D),
            full(wa), full(wc), full(wg), full(wo), full(cw), full(ng), full(lg), full(lb), full(rw3),
        ],
        out_specs=[row_spec(D), pl.BlockSpec((N_EXPERTS, T), lambda i: (0, i))],
        out_shape=[jax.ShapeDtypeStruct((S, D), F32), jax.ShapeDtypeStruct((N_EXPERTS, S), F32)],
        name="merge", compiler_params=_cparams(("parallel",)))(
            x, y_attn, conv, conv, conv, o_f, o_b, r, mg, wa, wc, wg, wo, cw, ng, lg, lb, rw3)


def _thr_body(k, aff_ref, thr_ref):
    a = aff_ref[...]

    def step(t, thr):
        cand = thr | jnp.left_shift(jnp.int32(1), 30 - t)
        ge = (a >= pltpu.bitcast(cand, F32)).astype(I32)
        cnt = jnp.sum(jnp.sum(ge, axis=1, keepdims=True), axis=2, keepdims=True)
        return jnp.where(cnt >= k, cand, thr)

    thr = lax.fori_loop(0, 31, step, jnp.zeros((N_EXPERTS, 1, 1), I32))
    thr_ref[...] = jnp.broadcast_to(thr, thr_ref.shape)


def _select_body(k, aff_ref, thr_ref, idx_ref, gate_ref, offs_ref):
    R = aff_ref.shape[0]
    L = LANES
    a = aff_ref[...]
    thr_bits = thr_ref[...]
    thr = pltpu.bitcast(thr_bits, F32)[0:1, 0:1]
    thr_next = pltpu.bitcast(thr_bits + 1, F32)[0:1, 0:1]
    gt = a >= thr_next
    eq = jnp.logical_and(a >= thr, jnp.logical_not(gt))
    n_gt = jnp.sum(jnp.sum(gt.astype(F32), axis=0, keepdims=True), axis=1, keepdims=True)
    need = float(k) - n_gt

    li = lax.broadcasted_iota(I32, (L, L), 0)
    lj = lax.broadcasted_iota(I32, (L, L), 1)
    upper = jnp.where(li <= lj, 1.0, 0.0).astype(BF16)
    ri = lax.broadcasted_iota(I32, (R, R), 0)
    rj = lax.broadcasted_iota(I32, (R, R), 1)
    lower_strict = jnp.where(rj < ri, 1.0, 0.0).astype(BF16)

    def prefix(mask_f):
        cum = jnp.dot(mask_f.astype(BF16), upper, preferred_element_type=F32)
        tot = jnp.broadcast_to(cum[:, L - 1:L], (R, L))
        offs = jnp.dot(lower_strict, tot.astype(BF16), preferred_element_type=F32)
        return cum, tot, offs

    eq_f = eq.astype(F32)
    cum_eq, _, offs_eq = prefix(eq_f)
    rank_eq = cum_eq - eq_f + offs_eq
    sel = jnp.logical_or(gt, jnp.logical_and(eq, rank_eq < need))
    sel_f = sel.astype(F32)
    cum, tot, offs = prefix(sel_f)
    offs_ref[...] = offs.astype(I32)

    offs_incl = offs + tot
    cum_t = cum.T.astype(BF16)
    a_t = a.T
    ah, am, al = _split3(a_t)
    row_iota = lax.broadcasted_iota(I32, (R, L), 0).astype(F32)
    lane_iota_col = lax.broadcasted_iota(I32, (L, L), 0).astype(F32)
    slot_lane = lax.broadcasted_iota(I32, (1, L), 1).astype(F32)

    def chunk(ci, carry):
        slot = slot_lane + lax.convert_element_type(ci * L, F32)
        row_of = jnp.sum((offs_incl <= slot).astype(F32), axis=0, keepdims=True)
        onehot = (row_iota == row_of).astype(F32)
        base = jnp.sum(onehot * offs, axis=0, keepdims=True)
        local = slot - base
        oh = onehot.astype(BF16)
        cum_at = jnp.dot(cum_t, oh, preferred_element_type=F32)
        lane_of = jnp.sum((cum_at <= local).astype(F32), axis=0, keepdims=True)
        val = (jnp.dot(ah, oh, preferred_element_type=F32) + jnp.dot(am, oh, preferred_element_type=F32)
               + jnp.dot(al, oh, preferred_element_type=F32))
        gate = jnp.sum(jnp.where(lane_iota_col == lane_of, val, 0.0), axis=0, keepdims=True)
        idx_ref[pl.ds(ci, 1), :] = (row_of * L + lane_of).astype(I32)
        gate_ref[pl.ds(ci, 1), :] = gate
        return carry

    lax.fori_loop(0, idx_ref.shape[0], chunk, 0)


def _topk(aff_t, k):
    E, S = aff_t.shape
    R = S // LANES
    aff3 = aff_t.reshape(E, R, LANES)
    thr = pl.pallas_call(
        functools.partial(_thr_body, k), grid=(1,),
        in_specs=[pl.BlockSpec((E, R, LANES), lambda i: (0, 0, 0))],
        out_specs=pl.BlockSpec((E, SUBLANES, LANES), lambda i: (0, 0, 0)),
        out_shape=jax.ShapeDtypeStruct((E, SUBLANES, LANES), I32), name="topk_thr",
        compiler_params=_cparams(("arbitrary",)))(aff3)
    n_chunks = k // LANES
    idx, gate, offs = pl.pallas_call(
        functools.partial(_select_body, k), grid=(E,),
        in_specs=[pl.BlockSpec((None, R, LANES), lambda e: (e, 0, 0)),
                  pl.BlockSpec((None, SUBLANES, LANES), lambda e: (e, 0, 0))],
        out_specs=[pl.BlockSpec((None, n_chunks, LANES), lambda e: (e, 0, 0)),
                   pl.BlockSpec((None, n_chunks, LANES), lambda e: (e, 0, 0)),
                   pl.BlockSpec((None, R, LANES), lambda e: (e, 0, 0))],
        out_shape=[jax.ShapeDtypeStruct((E, n_chunks, LANES), I32),
                   jax.ShapeDtypeStruct((E, n_chunks, LANES), F32),
                   jax.ShapeDtypeStruct((E, R, LANES), I32)],
        name="topk_select", compiler_params=_cparams(("parallel",)))(aff3, thr)
    return idx.reshape(E, k), gate.reshape(E, k), offs[:, :, 0]


FFN_TM = 1024
FFN_TF = 512


def _ffn_body(tm, idx_ref, idx_next_ref, x_hbm, gate_ref, wg_ref, wu_ref, wd_ref, y_ref, xs_ref, xb_ref, sem):
    f = pl.program_id(2)
    n_tiles = pl.num_programs(0) * pl.num_programs(1)
    tile = pl.program_id(0) * pl.num_programs(1) + pl.program_id(1)
    slot = tile % 2

    def row_copy(ids, j, s):
        return pltpu.make_async_copy(x_hbm.at[pl.ds(ids[0, j], 1), :], xs_ref.at[s, pl.ds(j, 1), :], sem.at[s])

    def start_rows(ids, s):
        def body(j, c):
            row_copy(ids, j, s).start()
            return c
        lax.fori_loop(0, tm, body, 0)

    @pl.when(f == 0)
    def _():
        @pl.when(tile == 0)
        def _():
            start_rows(idx_ref, slot)

        def wait_row(j, c):
            row_copy(idx_ref, j, slot).wait()
            return c
        lax.fori_loop(0, tm, wait_row, 0)

        @pl.when(tile + 1 < n_tiles)
        def _():
            start_rows(idx_next_ref, 1 - slot)

        xb_ref[...] = xs_ref[slot].astype(BF16)

    xb = xb_ref[...]
    hg = jnp.dot(xb, wg_ref[...].astype(BF16), preferred_element_type=F32)
    hu = jnp.dot(xb, wu_ref[...].astype(BF16), preferred_element_type=F32)
    hh = (hg * jax.nn.sigmoid(hg) * hu).astype(BF16)
    part = jnp.dot(hh, wd_ref[...].astype(BF16), preferred_element_type=F32)

    @pl.when(f == 0)
    def _():
        y_ref[...] = part

    @pl.when(f > 0)
    def _():
        y_ref[...] += part

    @pl.when(f == pl.num_programs(2) - 1)
    def _():
        y_ref[...] = y_ref[...] * gate_ref[...]


def _expert_ffn(x, idx, gate_col, w_gate, w_up, w_down, layer):
    E, cap = idx.shape
    D = x.shape[1]
    FF = w_gate.shape[-1]
    tm = min(FFN_TM, cap)
    mt = cap // tm
    n_tiles = E * mt
    idx3 = idx.reshape(n_tiles, 1, tm)
    smem_blk = lambda fn: pl.BlockSpec((None, 1, tm), fn, memory_space=pltpu.SMEM)
    return pl.pallas_call(
        functools.partial(_ffn_body, tm), grid=(E, mt, FF // FFN_TF),
        in_specs=[
            smem_blk(lambda e, m, f: (e * mt + m, 0, 0)),
            smem_blk(lambda e, m, f: (jnp.minimum(e * mt + m + 1, n_tiles - 1), 0, 0)),
            pl.BlockSpec(memory_space=pl.ANY),
            pl.BlockSpec((None, tm, 1), lambda e, m, f: (e, m, 0)),
            pl.BlockSpec((None, None, D, FFN_TF), lambda e, m, f: (layer, e, 0, f)),
            pl.BlockSpec((None, None, D, FFN_TF), lambda e, m, f: (layer, e, 0, f)),
            pl.BlockSpec((None, None, FFN_TF, D), lambda e, m, f: (layer, e, f, 0)),
        ],
        out_specs=pl.BlockSpec((None, tm, D), lambda e, m, f: (e, m, 0)),
        out_shape=jax.ShapeDtypeStruct((E, cap, D), F32),
        scratch_shapes=[pltpu.VMEM((2, tm, D), F32), pltpu.VMEM((tm, D), BF16), pltpu.SemaphoreType.DMA((2,))],
        name="ffn", compiler_params=_cparams(("arbitrary", "arbitrary", "arbitrary")))(
            idx3, idx3, x, gate_col, w_gate, w_up, w_down)


COMBINE_PASSES = 4
COMBINE_ROWS = 256


def _combine_body(cap, q_tokens, n_rows_total, start_ref, idx_ref, y_ref, acc_ref, buf_ref, sem):
    qi = pl.program_id(0)
    e = pl.program_id(1)

    @pl.when(e == 0)
    def _():
        acc_ref[...] = jnp.zeros_like(acc_ref)

    a = start_ref[e, qi]
    b = start_ref[e, qi + 1]
    n_chunks = (b - a + COMBINE_ROWS - 1) // COMBINE_ROWS
    tok0 = qi * q_tokens

    def chunk_src(ci):
        r0 = jnp.minimum(e * cap + a + ci * COMBINE_ROWS, n_rows_total - COMBINE_ROWS)
        return r0

    def copy(ci, slot):
        return pltpu.make_async_copy(y_ref.at[pl.ds(chunk_src(ci), COMBINE_ROWS)], buf_ref.at[slot], sem.at[slot])

    @pl.when(n_chunks > 0)
    def _():
        copy(0, 0).start()

    def chunk(ci, carry):
        slot = ci % 2

        @pl.when(ci + 1 < n_chunks)
        def _():
            copy(ci + 1, 1 - slot).start()

        copy(ci, slot).wait()
        lo = a + ci * COMBINE_ROWS
        hi = jnp.minimum(lo + COMBINE_ROWS, b)
        shift = e * cap - chunk_src(ci)

        def add(j, c):
            t = idx_ref[0, j] - tok0
            acc_ref[t] += buf_ref[slot, j + shift]
            return c

        lax.fori_loop(lo, hi, add, 0)
        return carry

    lax.fori_loop(0, n_chunks, chunk, 0)


def _combine(y_rows, idx, starts, S):
    E, cap = idx.shape
    q_tokens = S // COMBINE_PASSES
    n_rows_total = y_rows.shape[0]
    grid_spec = pltpu.PrefetchScalarGridSpec(
        num_scalar_prefetch=1, grid=(COMBINE_PASSES, E),
        in_specs=[pl.BlockSpec((None, 1, cap), lambda q, e, st: (e, 0, 0), memory_space=pltpu.SMEM),
                  pl.BlockSpec(memory_space=pl.ANY)],
        out_specs=pl.BlockSpec((q_tokens, SUBLANES, LANES), lambda q, e, st: (q, 0, 0)),
        scratch_shapes=[pltpu.VMEM((2, COMBINE_ROWS, SUBLANES, LANES), F32), pltpu.SemaphoreType.DMA((2,))])
    return pl.pallas_call(
        functools.partial(_combine_body, cap, q_tokens, n_rows_total), grid_spec=grid_spec,
        out_shape=jax.ShapeDtypeStruct((S, SUBLANES, LANES), F32), name="combine",
        compiler_params=_cparams(("arbitrary", "arbitrary")))(starts, idx.reshape(E, 1, cap), y_rows)


LN_TM = 512


def _ln_body(x_ref, f_ref, g_ref, b_ref, o_ref):
    o_ref[...] = _layer_norm(DN_ALPHA * x_ref[...] + f_ref[...], g_ref[...], b_ref[...])


def _residual_ln(x, f, g, b):
    S, D = x.shape
    row = pl.BlockSpec((LN_TM, D), lambda i: (i, 0))
    vec = pl.BlockSpec((1, D), lambda i: (0, 0))
    return pl.pallas_call(
        _ln_body, grid=(S // LN_TM,), in_specs=[row, row, vec, vec], out_specs=row,
        out_shape=jax.ShapeDtypeStruct((S, D), F32), name="ln",
        compiler_params=_cparams(("parallel",)))(x, f, g, b)


PROJ_TM = 512


def _pad_cols(w, n):
    return jnp.pad(w, ((0, 0), (0, n - w.shape[1])))


def _layer(x, p, experts, layer):
    S, D = x.shape
    w_in = p["w_in"].astype(BF16)
    c0 = ATT_Q + 2 * ATT_KV
    c1 = c0 + 3 * CONV_WIDTH
    c2 = c1 + 2 * GLA_KEY_DIM
    c3 = c2 + GLA_VALUE_DIM
    c4 = c3 + GLA_VALUE_DIM
    c5 = c4 + 2 * GLA_GATE_RANK
    ws = [w_in[:, :c0], w_in[:, c0:c1], w_in[:, c1:c2], w_in[:, c2:c3], w_in[:, c3:c4],
          _pad_cols(w_in[:, c4:c5], LANES)]
    qkv, conv, gqk, gv, gr, glr = _proj(x, ws, [BF16, F32, F32, F32, F32, F32], PROJ_TM, "proj_mix")
    (mg,) = _proj(x, [w_in[:, c5:]], [F32], PROJ_TM, "proj_gates")

    y_attn = _attention(qkv, p["attn_sink"])

    w2 = p["gla_gate_w2"].astype(BF16)
    rk = GLA_GATE_RANK
    w2_f = jnp.zeros((LANES, GLA_KEY_DIM), BF16).at[:rk].set(w2[0])
    w2_b = jnp.zeros((LANES, GLA_KEY_DIM), BF16).at[rk:2 * rk].set(w2[1])
    gb = p["gla_gate_b"].astype(F32)
    o_f = _gla_direction(gqk, gv, glr, w2_f, gb[0:1], False)
    o_b = _gla_direction(gqk, gv, glr, w2_b, gb[1:2], True)

    rw3 = jnp.stack(_split3(p["router_w"].T.astype(F32)))
    x1, aff_t = _merge(
        x, y_attn, conv, o_f, o_b, gr, mg,
        p["w_branch_attn"].astype(BF16), p["w_branch_conv"].astype(BF16), p["w_branch_gla"].astype(BF16),
        p["w_out"].astype(BF16), p["conv_w"].astype(F32), p["gla_norm_g"].astype(F32).reshape(1, -1),
        p["ln_mix_g"].astype(F32).reshape(1, -1), p["ln_mix_b"].astype(F32).reshape(1, -1), rw3)

    cap = CAPACITY_FACTOR * S // N_EXPERTS
    idx, gate, offs = _topk(aff_t, cap)
    y = _expert_ffn(x1, idx, gate.reshape(N_EXPERTS, cap, 1), experts["expert_w_gate"], experts["expert_w_up"],
                    experts["expert_w_down"], layer)
    rows_per_pass = S // COMBINE_PASSES // LANES
    starts = jnp.concatenate([offs[:, ::rows_per_pass], jnp.full((N_EXPERTS, 1), cap, I32)], axis=1)
    ffn = _combine(y.reshape(N_EXPERTS * cap, SUBLANES, LANES), idx, starts, S).reshape(S, D)
    return _residual_ln(x1, ffn, p["ln_ffn_g"].astype(F32).reshape(1, -1), p["ln_ffn_b"].astype(F32).reshape(1, -1))


def kernel(x, w_in, attn_sink, conv_w, gla_gate_w2, gla_gate_b, gla_norm_g, w_branch_attn, w_branch_conv,
           w_branch_gla, w_out, ln_mix_g, ln_mix_b, router_w, expert_w_gate, expert_w_up, expert_w_down,
           ln_ffn_g, ln_ffn_b):
    params = dict(w_in=w_in, attn_sink=attn_sink, conv_w=conv_w, gla_gate_w2=gla_gate_w2, gla_gate_b=gla_gate_b,
                  gla_norm_g=gla_norm_g, w_branch_attn=w_branch_attn, w_branch_conv=w_branch_conv,
                  w_branch_gla=w_branch_gla, w_out=w_out, ln_mix_g=ln_mix_g, ln_mix_b=ln_mix_b,
                  router_w=router_w, expert_w_gate=expert_w_gate, expert_w_up=expert_w_up,
                  expert_w_down=expert_w_down, ln_ffn_g=ln_ffn_g, ln_ffn_b=ln_ffn_b)
    B, S, D = x.shape
    outs = []
    for bi in range(B):
        h = x[bi]
        for l in range(DEPTH):
            h = _layer(h, {name: val[l] for name, val in params.items() if not name.startswith("expert_")},
                       params, l)
        outs.append(h)
    return jnp.stack(outs)
```

```python
import functools
import math

import numpy as np
import jax
import jax.numpy as jnp
from jax import lax
from jax.experimental import pallas as pl
from jax.experimental.pallas import tpu as pltpu

F32 = jnp.float32
BF16 = jnp.bfloat16
I32 = jnp.int32

D_MODEL = 1024
DEPTH = 2
ATT_HEADS, ATT_KV_HEADS, ATT_HEAD_DIM, ATT_BLOCK, ATT_WINDOW = 8, 2, 64, 128, 128
ATT_GROUP = ATT_HEADS // ATT_KV_HEADS
CONV_WIDTH = 512
GLA_HEADS, GLA_KEY_DIM, GLA_VALUE_DIM, GLA_GATE_RANK, GLA_TAU = 4, 256, 512, 16, 16.0
GLA_DK = GLA_KEY_DIM // GLA_HEADS
GLA_DV = GLA_VALUE_DIM // GLA_HEADS
N_EXPERTS, EXPERT_FF, CAPACITY_FACTOR = 16, 2048, 2
DN_ALPHA = (2 * DEPTH) ** 0.25
LN_EPS = 1e-5
RMS_EPS = 1e-6
ATT_Q, ATT_KV = ATT_HEADS * ATT_HEAD_DIM, ATT_KV_HEADS * ATT_HEAD_DIM

LANES = 128
SUBLANES = 8
NEG_BIG = -1e30

GLA_CHUNK = 128
VMEM_LIMIT = 56 * 1024 * 1024


def _cparams(sem, vmem=VMEM_LIMIT):
    return pltpu.CompilerParams(dimension_semantics=sem, vmem_limit_bytes=vmem)


def _proj_body(n_out, x_ref, *refs):
    xb = x_ref[...].astype(BF16)
    for w_ref, o_ref in zip(refs[:n_out], refs[n_out:]):
        o_ref[...] = jnp.dot(xb, w_ref[...], preferred_element_type=F32).astype(o_ref.dtype)


def _proj(x, ws, out_dtypes, tm, name):
    S, D = x.shape
    in_specs = [pl.BlockSpec((tm, D), lambda i: (i, 0))]
    in_specs += [pl.BlockSpec(w.shape, lambda i: (0, 0)) for w in ws]
    out_specs = [pl.BlockSpec((tm, w.shape[1]), lambda i: (i, 0)) for w in ws]
    out_shape = [jax.ShapeDtypeStruct((S, w.shape[1]), dt) for w, dt in zip(ws, out_dtypes)]
    return pl.pallas_call(
        functools.partial(_proj_body, len(ws)), grid=(S // tm,), in_specs=in_specs,
        out_specs=out_specs, out_shape=out_shape, name=name,
        compiler_params=_cparams(("parallel",)))(x, *ws)


ATT_TQ = 4 * ATT_BLOCK


def _attn_bias_table():
    q_off = np.arange(ATT_BLOCK)[:, None]
    k_off = np.arange(3 * ATT_BLOCK)[None, :] - ATT_BLOCK
    dist = np.abs(q_off - k_off).astype(np.float32)
    slopes = 2.0 ** (-8.0 * np.arange(1, ATT_HEADS + 1, dtype=np.float32) / ATT_HEADS)
    return np.stack([np.where(dist <= ATT_WINDOW, -sl * dist, NEG_BIG) for sl in slopes]).astype(np.float32)


def _attn_body(n_steps, sink_ref, cur_ref, prev_ref, next_ref, bias_ref, o_ref):
    i = pl.program_id(0)
    hd = ATT_HEAD_DIM
    per = ATT_TQ // ATT_BLOCK
    kv_all = jnp.concatenate(
        [prev_ref[:, ATT_Q:], cur_ref[:, ATT_Q:], next_ref[:, ATT_Q:]], axis=0)
    col = lax.broadcasted_iota(I32, (1, 3 * ATT_BLOCK), 1)
    for b in range(per):
        rows = slice(b * ATT_BLOCK, (b + 1) * ATT_BLOCK)
        kvb = kv_all[b * ATT_BLOCK:(b + 3) * ATT_BLOCK]
        edge = None
        if b == 0:
            edge = jnp.where(jnp.logical_and(col < ATT_BLOCK, i == 0), NEG_BIG, 0.0).astype(F32)
        if b == per - 1:
            last = jnp.where(jnp.logical_and(col >= 2 * ATT_BLOCK, i == n_steps - 1), NEG_BIG, 0.0).astype(F32)
            edge = last if edge is None else edge + last
        scores = []
        for h in range(ATT_HEADS):
            j = h // ATT_GROUP
            q = cur_ref[rows, h * hd:(h + 1) * hd]
            kj = kvb[:, j * hd:(j + 1) * hd]
            s = lax.dot_general(q, kj, (((1,), (1,)), ((), ())), preferred_element_type=F32) + bias_ref[h]
            scores.append(s if edge is None else s + edge)
        probs, scales = [], []
        for h in range(ATT_HEADS):
            sink = sink_ref[h]
            m = jnp.maximum(jnp.max(scores[h], axis=-1, keepdims=True), sink)
            p = jnp.exp(scores[h] - m)
            denom = jnp.sum(p, axis=-1, keepdims=True) + jnp.exp(sink - m)
            probs.append(p.astype(BF16))
            scales.append(1.0 / denom)
        outs = []
        for h in range(ATT_HEADS):
            j = h // ATT_GROUP
            vj = kvb[:, ATT_KV + j * hd:ATT_KV + (j + 1) * hd]
            outs.append(jnp.dot(probs[h], vj, preferred_element_type=F32) * scales[h])
        o_ref[rows, :] = jnp.concatenate(outs, axis=-1).astype(o_ref.dtype)


def _attention(qkv, sink):
    S = qkv.shape[0]
    n_blocks = S // ATT_BLOCK
    per = ATT_TQ // ATT_BLOCK
    n_steps = S // ATT_TQ
    width = qkv.shape[1]
    bias = jnp.asarray(_attn_bias_table())
    grid_spec = pltpu.PrefetchScalarGridSpec(
        num_scalar_prefetch=1, grid=(n_steps,),
        in_specs=[
            pl.BlockSpec((ATT_TQ, width), lambda i, sk: (i, 0)),
            pl.BlockSpec((ATT_BLOCK, width), lambda i, sk: (jnp.maximum(i * per - 1, 0), 0)),
            pl.BlockSpec((ATT_BLOCK, width), lambda i, sk: (jnp.minimum((i + 1) * per, n_blocks - 1), 0)),
            pl.BlockSpec(bias.shape, lambda i, sk: (0, 0, 0)),
        ],
        out_specs=pl.BlockSpec((ATT_TQ, ATT_Q), lambda i, sk: (i, 0)))
    return pl.pallas_call(
        functools.partial(_attn_body, n_steps), grid_spec=grid_spec,
        out_shape=jax.ShapeDtypeStruct((S, ATT_Q), BF16), name="attn",
        compiler_params=_cparams(("parallel",)))(sink.astype(F32), qkv, qkv, qkv, bias)


GLA_TB = 512


def _mid_broadcast(x, w, pos, reverse):
    C, N = x.shape
    if 2 * w >= SUBLANES:
        x3 = x.reshape(C // (2 * w), 2 * w, N)
        r = w if reverse else w - 1
        return jnp.broadcast_to(x3[:, r:r + 1, :], x3.shape).reshape(C, N)
    out = jnp.zeros_like(x)
    for o in range(-(w - 1), w + 1):
        shift = (-o if reverse else o) % C
        src = x if shift == 0 else pltpu.roll(x, shift, 0)
        out = jnp.where(pos - (w - 1) == o, src, out)
    return out


def _gla_body(reverse, qk_ref, v_ref, lr_ref, w2_ref, gb_ref, o_ref, state_ref):
    C = GLA_CHUNK
    H, dk, dv = GLA_HEADS, GLA_DK, GLA_DV

    @pl.when(pl.program_id(0) == 0)
    def _():
        state_ref[...] = jnp.zeros_like(state_ref)

    row = lax.broadcasted_iota(I32, (C, 1), 0)
    rowi = lax.broadcasted_iota(I32, (C, C), 0)
    coli = lax.broadcasted_iota(I32, (C, C), 1)
    eye = rowi == coli

    def scan_pos(idx, w):
        p = idx & (2 * w - 1)
        return (2 * w - 1) - p if reverse else p

    levels = []
    w = 1
    while w < C:
        shift = int(math.log2(2 * w))
        pair = jnp.logical_and(jnp.right_shift(rowi, shift) == jnp.right_shift(coli, shift),
                               jnp.logical_and(scan_pos(rowi, w) >= w, scan_pos(coli, w) < w))
        pos = scan_pos(row, w)
        levels.append((w, pos, pos >= w, jnp.where(pair, 1.0, 0.0).astype(F32)))
        w *= 2

    def prepare(c):
        rows = slice(c * C, (c + 1) * C)
        q = qk_ref[rows, :GLA_KEY_DIM] * (dk ** -0.5)
        k = qk_ref[rows, GLA_KEY_DIM:]
        z = jnp.dot(lr_ref[rows, :].astype(BF16), w2_ref[...], preferred_element_type=F32) + gb_ref[...]
        la = (jnp.minimum(z, 0.0) - jnp.log(1.0 + jnp.exp(-jnp.abs(z)))) * (1.0 / GLA_TAU)
        cw = la
        scores = [None] * H
        for w, pos, second, pair in levels:
            mid = _mid_broadcast(cw, w, pos, reverse)
            fac = jnp.exp(jnp.where(second, cw, mid - cw))
            t = (jnp.where(second, q, k) * fac).astype(BF16)
            for h in range(H):
                th = t[:, h * dk:(h + 1) * dk]
                sc = lax.dot_general(th, th, (((1,), (1,)), ((), ())), preferred_element_type=F32)
                sc = sc * pair
                scores[h] = sc if scores[h] is None else scores[h] + sc
            cw = cw + jnp.where(second, mid, 0.0)
        b = cw
        btot = b[0:1, :] if reverse else b[C - 1:C, :]
        qhat = (q * jnp.exp(b)).astype(BF16)
        khat = (k * jnp.exp(btot - b)).astype(BF16)
        qk_diag = q * k
        a_mats = []
        for h in range(H):
            diag = jnp.sum(qk_diag[:, h * dk:(h + 1) * dk], axis=-1, keepdims=True)
            a_mats.append((scores[h] + jnp.where(eye, diag, 0.0)).astype(BF16))
        return rows, a_mats, qhat, khat, jnp.exp(btot)

    def finish(prep):
        rows, a_mats, qhat, khat, a_tot = prep
        v = v_ref[rows, :]
        for h in range(H):
            ksl = slice(h * dk, (h + 1) * dk)
            vsl = slice(h * dv, (h + 1) * dv)
            vh = v[:, vsl]
            st = state_ref[h]
            o_h = jnp.dot(a_mats[h], vh.astype(BF16), preferred_element_type=F32)
            o_h = o_h + lax.dot_general(qhat[:, ksl], st.astype(BF16), (((1,), (1,)), ((), ())),
                                        preferred_element_type=F32)
            o_ref[rows, vsl] = o_h
            upd = jnp.dot(vh.T.astype(BF16), khat[:, ksl], preferred_element_type=F32)
            state_ref[h] = st * a_tot[:, ksl] + upd

    n_chunks = GLA_TB // C
    order = list(range(n_chunks - 1, -1, -1) if reverse else range(n_chunks))
    prep = prepare(order[0])
    for nxt in order[1:]:
        prep_next = prepare(nxt)
        finish(prep)
        prep = prep_next
    finish(prep)


def _gla_direction(qk, v, lr, w2p, gb, reverse):
    S = qk.shape[0]
    n = S // GLA_TB
    blk = (lambda i: (n - 1 - i, 0)) if reverse else (lambda i: (i, 0))
    return pl.pallas_call(
        functools.partial(_gla_body, reverse), grid=(n,),
        in_specs=[
            pl.BlockSpec((GLA_TB, 2 * GLA_KEY_DIM), blk),
            pl.BlockSpec((GLA_TB, GLA_VALUE_DIM), blk),
            pl.BlockSpec((GLA_TB, LANES), blk),
            pl.BlockSpec(w2p.shape, lambda i: (0, 0)),
            pl.BlockSpec(gb.shape, lambda i: (0, 0)),
        ],
        out_specs=pl.BlockSpec((GLA_TB, GLA_VALUE_DIM), blk),
        out_shape=jax.ShapeDtypeStruct((S, GLA_VALUE_DIM), F32),
        scratch_shapes=[pltpu.VMEM((GLA_HEADS, GLA_DV, GLA_DK), F32)], name="gla",
        compiler_params=_cparams(("arbitrary",)))(qk, v, lr, w2p, gb)


MERGE_TM = 256


def _layer_norm(h, g, b):
    mu = jnp.mean(h, axis=-1, keepdims=True)
    d = h - mu
    var = jnp.mean(d * d, axis=-1, keepdims=True)
    return d * lax.rsqrt(var + LN_EPS) * g + b


def _sigmoid(x):
    return 0.5 * jnp.tanh(0.5 * x) + 0.5


def _split3(x):
    hi = x.astype(BF16)
    r1 = x - hi.astype(F32)
    mid = r1.astype(BF16)
    lo = (r1 - mid.astype(F32)).astype(BF16)
    return hi, mid, lo


def _merge_body(n_tiles, x_ref, ya_ref, cv_ref, cvp_ref, cvn_ref, of_ref, ob_ref, r_ref, mg_ref,
                wa_ref, wc_ref, wg_ref, wo_ref, cw_ref, ng_ref, lg_ref, lb_ref, rw_ref,
                x1_ref, aff_ref):
    i = pl.program_id(0)
    T = MERGE_TM
    CW = CONV_WIDTH
    u = cv_ref[:, 2 * CW:] * cv_ref[:, :CW]
    u_before = cvp_ref[SUBLANES - 1:SUBLANES, 2 * CW:] * cvp_ref[SUBLANES - 1:SUBLANES, :CW]
    u_after = cvn_ref[0:1, 2 * CW:] * cvn_ref[0:1, :CW]
    u_before = jnp.where(i == 0, 0.0, u_before)
    u_after = jnp.where(i == n_tiles - 1, 0.0, u_after)
    row = lax.broadcasted_iota(I32, (T, 1), 0)
    u_m1 = jnp.where(row == 0, u_before, pltpu.roll(u, 1, 0))
    u_p1 = jnp.where(row == T - 1, u_after, pltpu.roll(u, T - 1, 0))
    y_conv = cv_ref[:, CW:2 * CW] * (cw_ref[0:1, :] * u_m1 + cw_ref[1:2, :] * u + cw_ref[2:3, :] * u_p1)

    o = of_ref[...] + ob_ref[...]
    parts = []
    for h in range(GLA_HEADS):
        oh = o[:, h * GLA_DV:(h + 1) * GLA_DV]
        parts.append(oh * lax.rsqrt(jnp.mean(oh * oh, axis=-1, keepdims=True) + RMS_EPS))
    r = r_ref[...]
    y_gla = jnp.concatenate(parts, axis=-1) * ng_ref[...] * (r * _sigmoid(r))

    D = D_MODEL
    merged = _sigmoid(mg_ref[:, :D]) * jnp.dot(ya_ref[...], wa_ref[...], preferred_element_type=F32)
    merged += _sigmoid(mg_ref[:, D:2 * D]) * jnp.dot(y_conv.astype(BF16), wc_ref[...], preferred_element_type=F32)
    merged += _sigmoid(mg_ref[:, 2 * D:]) * jnp.dot(y_gla.astype(BF16), wg_ref[...], preferred_element_type=F32)
    hres = DN_ALPHA * x_ref[...] + jnp.dot(merged.astype(BF16), wo_ref[...], preferred_element_type=F32)
    x1 = _layer_norm(hres, lg_ref[...], lb_ref[...])
    x1_ref[...] = x1

    nt = (((1,), (1,)), ((), ()))
    xh = x1.astype(BF16)
    xl = (x1 - xh.astype(F32)).astype(BF16)
    both = lax.dot_general(rw_ref[...], xh, nt, preferred_element_type=F32)
    logits = (both[:N_EXPERTS] + both[N_EXPERTS:]
              + lax.dot_general(rw_ref[:N_EXPERTS], xl, nt, preferred_element_type=F32))
    mx = jnp.max(logits, axis=0, keepdims=True)
    ex = jnp.exp(logits - mx)
    aff_ref[...] = ex / jnp.sum(ex, axis=0, keepdims=True)


def _merge(x, y_attn, conv, o_f, o_b, r, mg, wa, wc, wg, wo, cw, ng, lg, lb, rw3):
    S, D = x.shape
    T = MERGE_TM
    n_tiles = S // T
    per = T // SUBLANES
    n8 = S // SUBLANES
    row_spec = lambda wd: pl.BlockSpec((T, wd), lambda i: (i, 0))
    full = lambda a: pl.BlockSpec(a.shape, lambda i: (0,) * a.ndim)
    return pl.pallas_call(
        functools.partial(_merge_body, n_tiles), grid=(n_tiles,),
        in_specs=[
            row_spec(D), row_spec(ATT_Q), row_spec(3 * CONV_WIDTH),
            pl.BlockSpec((SUBLANES, 3 * CONV_WIDTH), lambda i: (jnp.maximum(i * per - 1, 0), 0)),
            pl.BlockSpec((SUBLANES, 3 * CONV_WIDTH), lambda i: (jnp.minimum((i + 1) * per, n8 - 1), 0)),
            row_spec(GLA_VALUE_DIM), row_spec(GLA_VALUE_DIM), row_spec(GLA_VALUE_DIM), row_spec(3 * D),
            full(wa), full(wc), full(wg), full(wo), full(cw), full(ng), full(lg), full(lb), full(rw3),
        ],
        out_specs=[row_spec(D), pl.BlockSpec((N_EXPERTS, T), lambda i: (0, i))],
        out_shape=[jax.ShapeDtypeStruct((S, D), F32), jax.ShapeDtypeStruct((N_EXPERTS, S), F32)],
        name="merge", compiler_params=_cparams(("parallel",)))(
            x, y_attn, conv, conv, conv, o_f, o_b, r, mg, wa, wc, wg, wo, cw, ng, lg, lb, rw3)


def _thr_body(k, aff_ref, thr_ref):
    a = aff_ref[...]

    def step(t, thr):
        cand = thr | jnp.left_shift(jnp.int32(1), 30 - t)
        ge = (a >= pltpu.bitcast(cand, F32)).astype(I32)
        cnt = jnp.sum(jnp.sum(ge, axis=1, keepdims=True), axis=2, keepdims=True)
        return jnp.where(cnt >= k, cand, thr)

    thr = lax.fori_loop(0, 31, step, jnp.zeros((N_EXPERTS, 1, 1), I32))
    thr_ref[...] = jnp.broadcast_to(thr, thr_ref.shape)


def _select_body(k, aff_ref, thr_ref, idx_ref, gate_ref, offs_ref):
    R = aff_ref.shape[0]
    L = LANES
    a = aff_ref[...]
    thr_bits = thr_ref[...]
    thr = pltpu.bitcast(thr_bits, F32)[0:1, 0:1]
    thr_next = pltpu.bitcast(thr_bits + 1, F32)[0:1, 0:1]
    gt = a >= thr_next
    eq = jnp.logical_and(a >= thr, jnp.logical_not(gt))
    n_gt = jnp.sum(jnp.sum(gt.astype(F32), axis=0, keepdims=True), axis=1, keepdims=True)
    need = float(k) - n_gt

    li = lax.broadcasted_iota(I32, (L, L), 0)
    lj = lax.broadcasted_iota(I32, (L, L), 1)
    upper = jnp.where(li <= lj, 1.0, 0.0).astype(BF16)
    ri = lax.broadcasted_iota(I32, (R, R), 0)
    rj = lax.broadcasted_iota(I32, (R, R), 1)
    lower_strict = jnp.where(rj < ri, 1.0, 0.0).astype(BF16)

    def prefix(mask_f):
        cum = jnp.dot(mask_f.astype(BF16), upper, preferred_element_type=F32)
        tot = jnp.broadcast_to(cum[:, L - 1:L], (R, L))
        offs = jnp.dot(lower_strict, tot.astype(BF16), preferred_element_type=F32)
        return cum, tot, offs

    eq_f = eq.astype(F32)
    cum_eq, _, offs_eq = prefix(eq_f)
    rank_eq = cum_eq - eq_f + offs_eq
    sel = jnp.logical_or(gt, jnp.logical_and(eq, rank_eq < need))
    sel_f = sel.astype(F32)
    cum, tot, offs = prefix(sel_f)
    offs_ref[...] = offs.astype(I32)

    offs_incl = offs + tot
    cum_t = cum.T.astype(BF16)
    a_t = a.T
    ah, am, al = _split3(a_t)
    row_iota = lax.broadcasted_iota(I32, (R, L), 0).astype(F32)
    lane_iota_col = lax.broadcasted_iota(I32, (L, L), 0).astype(F32)
    slot_lane = lax.broadcasted_iota(I32, (1, L), 1).astype(F32)

    def chunk(ci, carry):
        slot = slot_lane + lax.convert_element_type(ci * L, F32)
        row_of = jnp.sum((offs_incl <= slot).astype(F32), axis=0, keepdims=True)
        onehot = (row_iota == row_of).astype(F32)
        base = jnp.sum(onehot * offs, axis=0, keepdims=True)
        local = slot - base
        oh = onehot.astype(BF16)
        cum_at = jnp.dot(cum_t, oh, preferred_element_type=F32)
        lane_of = jnp.sum((cum_at <= local).astype(F32), axis=0, keepdims=True)
        val = (jnp.dot(ah, oh, preferred_element_type=F32) + jnp.dot(am, oh, preferred_element_type=F32)
               + jnp.dot(al, oh, preferred_element_type=F32))
        gate = jnp.sum(jnp.where(lane_iota_col == lane_of, val, 0.0), axis=0, keepdims=True)
        idx_ref[pl.ds(ci, 1), :] = (row_of * L + lane_of).astype(I32)
        gate_ref[pl.ds(ci, 1), :] = gate
        return carry

    lax.fori_loop(0, idx_ref.shape[0], chunk, 0)


def _topk(aff_t, k):
    E, S = aff_t.shape
    R = S // LANES
    aff3 = aff_t.reshape(E, R, LANES)
    thr = pl.pallas_call(
        functools.partial(_thr_body, k), grid=(1,),
        in_specs=[pl.BlockSpec((E, R, LANES), lambda i: (0, 0, 0))],
        out_specs=pl.BlockSpec((E, SUBLANES, LANES), lambda i: (0, 0, 0)),
        out_shape=jax.ShapeDtypeStruct((E, SUBLANES, LANES), I32), name="topk_thr",
        compiler_params=_cparams(("arbitrary",)))(aff3)
    n_chunks = k // LANES
    idx, gate, offs = pl.pallas_call(
        functools.partial(_select_body, k), grid=(E,),
        in_specs=[pl.BlockSpec((None, R, LANES), lambda e: (e, 0, 0)),
                  pl.BlockSpec((None, SUBLANES, LANES), lambda e: (e, 0, 0))],
        out_specs=[pl.BlockSpec((None, n_chunks, LANES), lambda e: (e, 0, 0)),
                   pl.BlockSpec((None, n_chunks, LANES), lambda e: (e, 0, 0)),
                   pl.BlockSpec((None, R, LANES), lambda e: (e, 0, 0))],
        out_shape=[jax.ShapeDtypeStruct((E, n_chunks, LANES), I32),
                   jax.ShapeDtypeStruct((E, n_chunks, LANES), F32),
                   jax.ShapeDtypeStruct((E, R, LANES), I32)],
        name="topk_select", compiler_params=_cparams(("parallel",)))(aff3, thr)
    return idx.reshape(E, k), gate.reshape(E, k), offs[:, :, 0]


FFN_TM = 1024
FFN_TF = 512


def _ffn_body(tm, n_f, idx_ref, idx_next_ref, x_hbm, gate_ref, wg_ref, wu_ref, wd_ref, y_ref, xs_ref, xb_ref,
              sem):
    f = pl.program_id(2)
    n_tiles = pl.num_programs(0) * pl.num_programs(1)
    tile = pl.program_id(0) * pl.num_programs(1) + pl.program_id(1)
    slot = tile % 2
    per_step = tm // n_f

    def row_copy(ids, j, s):
        return pltpu.make_async_copy(x_hbm.at[pl.ds(ids[0, j], 1), :], xs_ref.at[s, pl.ds(j, 1), :], sem.at[s])

    def wait_tile(s):
        pltpu.make_async_copy(x_hbm.at[pl.ds(0, tm), :], xs_ref.at[s], sem.at[s]).wait()

    @pl.when(f == 0)
    def _():
        @pl.when(tile == 0)
        def _():
            def body(j, c):
                row_copy(idx_ref, j, slot).start()
                return c
            lax.fori_loop(0, tm, body, 0)

        wait_tile(slot)
        xb_ref[...] = xs_ref[slot].astype(BF16)

    for j in range(per_step):
        row_copy(idx_next_ref, f * per_step + j, 1 - slot).start()

    xb = xb_ref[...]
    hg = jnp.dot(xb, wg_ref[...].astype(BF16), preferred_element_type=F32)
    hu = jnp.dot(xb, wu_ref[...].astype(BF16), preferred_element_type=F32)
    hh = (hg * jax.nn.sigmoid(hg) * hu).astype(BF16)
    part = jnp.dot(hh, wd_ref[...].astype(BF16), preferred_element_type=F32)

    @pl.when(f == 0)
    def _():
        y_ref[...] = part

    @pl.when(f > 0)
    def _():
        y_ref[...] += part

    @pl.when(f == n_f - 1)
    def _():
        y_ref[...] = y_ref[...] * gate_ref[...]

        @pl.when(tile == n_tiles - 1)
        def _():
            wait_tile(1 - slot)


def _expert_ffn(x, idx, gate_col, w_gate, w_up, w_down, layer):
    E, cap = idx.shape
    D = x.shape[1]
    FF = w_gate.shape[-1]
    tm = min(FFN_TM, cap)
    mt = cap // tm
    n_tiles = E * mt
    idx3 = idx.reshape(n_tiles, 1, tm)
    smem_blk = lambda fn: pl.BlockSpec((None, 1, tm), fn, memory_space=pltpu.SMEM)
    return pl.pallas_call(
        functools.partial(_ffn_body, tm, FF // FFN_TF), grid=(E, mt, FF // FFN_TF),
        in_specs=[
            smem_blk(lambda e, m, f: (e * mt + m, 0, 0)),
            smem_blk(lambda e, m, f: (jnp.minimum(e * mt + m + 1, n_tiles - 1), 0, 0)),
            pl.BlockSpec(memory_space=pl.ANY),
            pl.BlockSpec((None, tm, 1), lambda e, m, f: (e, m, 0)),
            pl.BlockSpec((None, None, D, FFN_TF), lambda e, m, f: (layer, e, 0, f)),
            pl.BlockSpec((None, None, D, FFN_TF), lambda e, m, f: (layer, e, 0, f)),
            pl.BlockSpec((None, None, FFN_TF, D), lambda e, m, f: (layer, e, f, 0)),
        ],
        out_specs=pl.BlockSpec((None, tm, D), lambda e, m, f: (e, m, 0)),
        out_shape=jax.ShapeDtypeStruct((E, cap, D), F32),
        scratch_shapes=[pltpu.VMEM((2, tm, D), F32), pltpu.VMEM((tm, D), BF16), pltpu.SemaphoreType.DMA((2,))],
        name="ffn", compiler_params=_cparams(("arbitrary", "arbitrary", "arbitrary")))(
            idx3, idx3, x, gate_col, w_gate, w_up, w_down)


COMBINE_PASSES = 4
COMBINE_ROWS = 256
COMBINE_UNROLL = 8


def _combine_body(cap, q_tokens, n_rows_total, start_ref, idx_ref, y_ref, acc_ref, buf_ref, sem):
    qi = pl.program_id(0)
    e = pl.program_id(1)

    @pl.when(e == 0)
    def _():
        acc_ref[...] = jnp.zeros_like(acc_ref)

    a = start_ref[e, qi]
    b = start_ref[e, qi + 1]
    n_chunks = (b - a + COMBINE_ROWS - 1) // COMBINE_ROWS
    tok0 = qi * q_tokens

    def chunk_src(ci):
        r0 = jnp.minimum(e * cap + a + ci * COMBINE_ROWS, n_rows_total - COMBINE_ROWS)
        return r0

    def copy(ci, slot):
        return pltpu.make_async_copy(y_ref.at[pl.ds(chunk_src(ci), COMBINE_ROWS)], buf_ref.at[slot], sem.at[slot])

    @pl.when(n_chunks > 0)
    def _():
        copy(0, 0).start()

    def chunk(ci, carry):
        slot = ci % 2

        @pl.when(ci + 1 < n_chunks)
        def _():
            copy(ci + 1, 1 - slot).start()

        copy(ci, slot).wait()
        lo = a + ci * COMBINE_ROWS
        hi = jnp.minimum(lo + COMBINE_ROWS, b)
        shift = e * cap - chunk_src(ci)

        def add_rows(j0, n):
            ts = [idx_ref[0, j0 + u] - tok0 for u in range(n)]
            sums = [acc_ref[ts[u]] + buf_ref[slot, j0 + u + shift] for u in range(n)]
            for u in range(n):
                acc_ref[ts[u]] = sums[u]

        n_groups = (hi - lo) // COMBINE_UNROLL

        def add_group(g, c):
            add_rows(lo + g * COMBINE_UNROLL, COMBINE_UNROLL)
            return c

        def add_one(j, c):
            add_rows(j, 1)
            return c

        lax.fori_loop(0, n_groups, add_group, 0)
        lax.fori_loop(lo + n_groups * COMBINE_UNROLL, hi, add_one, 0)
        return carry

    lax.fori_loop(0, n_chunks, chunk, 0)


def _combine(y_rows, idx, starts, S):
    E, cap = idx.shape
    q_tokens = S // COMBINE_PASSES
    n_rows_total = y_rows.shape[0]
    grid_spec = pltpu.PrefetchScalarGridSpec(
        num_scalar_prefetch=1, grid=(COMBINE_PASSES, E),
        in_specs=[pl.BlockSpec((None, 1, cap), lambda q, e, st: (e, 0, 0), memory_space=pltpu.SMEM),
                  pl.BlockSpec(memory_space=pl.ANY)],
        out_specs=pl.BlockSpec((q_tokens, SUBLANES, LANES), lambda q, e, st: (q, 0, 0)),
        scratch_shapes=[pltpu.VMEM((2, COMBINE_ROWS, SUBLANES, LANES), F32), pltpu.SemaphoreType.DMA((2,))])
    return pl.pallas_call(
        functools.partial(_combine_body, cap, q_tokens, n_rows_total), grid_spec=grid_spec,
        out_shape=jax.ShapeDtypeStruct((S, SUBLANES, LANES), F32), name="combine",
        compiler_params=_cparams(("arbitrary", "arbitrary")))(starts, idx.reshape(E, 1, cap), y_rows)


LN_TM = 512


def _ln_body(x_ref, f_ref, g_ref, b_ref, o_ref):
    o_ref[...] = _layer_norm(DN_ALPHA * x_ref[...] + f_ref[...], g_ref[...], b_ref[...])


def _residual_ln(x, f, g, b):
    S, D = x.shape
    row = pl.BlockSpec((LN_TM, D), lambda i: (i, 0))
    vec = pl.BlockSpec((1, D), lambda i: (0, 0))
    return pl.pallas_call(
        _ln_body, grid=(S // LN_TM,), in_specs=[row, row, vec, vec], out_specs=row,
        out_shape=jax.ShapeDtypeStruct((S, D), F32), name="ln",
        compiler_params=_cparams(("parallel",)))(x, f, g, b)


PROJ_TM = 512


def _pad_cols(w, n):
    return jnp.pad(w, ((0, 0), (0, n - w.shape[1])))


def _layer(x, p, experts, layer):
    S, D = x.shape
    w_in = p["w_in"].astype(BF16)
    c0 = ATT_Q + 2 * ATT_KV
    c1 = c0 + 3 * CONV_WIDTH
    c2 = c1 + 2 * GLA_KEY_DIM
    c3 = c2 + GLA_VALUE_DIM
    c4 = c3 + GLA_VALUE_DIM
    c5 = c4 + 2 * GLA_GATE_RANK
    w_qkv = jnp.concatenate([w_in[:, :ATT_Q] * (ATT_HEAD_DIM ** -0.5), w_in[:, ATT_Q:c0]], axis=1)
    ws = [w_qkv, w_in[:, c0:c1], w_in[:, c1:c2], w_in[:, c2:c3], w_in[:, c3:c4],
          _pad_cols(w_in[:, c4:c5], LANES)]
    qkv, conv, gqk, gv, gr, glr = _proj(x, ws, [BF16, F32, F32, F32, F32, F32], PROJ_TM, "proj_mix")
    (mg,) = _proj(x, [w_in[:, c5:]], [F32], PROJ_TM, "proj_gates")

    y_attn = _attention(qkv, p["attn_sink"])

    w2 = p["gla_gate_w2"].astype(BF16)
    rk = GLA_GATE_RANK
    w2_f = jnp.zeros((LANES, GLA_KEY_DIM), BF16).at[:rk].set(w2[0])
    w2_b = jnp.zeros((LANES, GLA_KEY_DIM), BF16).at[rk:2 * rk].set(w2[1])
    gb = p["gla_gate_b"].astype(F32)
    o_f = _gla_direction(gqk, gv, glr, w2_f, gb[0:1], False)
    o_b = _gla_direction(gqk, gv, glr, w2_b, gb[1:2], True)

    rw_t = p["router_w"].T.astype(F32)
    rw_hi = rw_t.astype(BF16)
    rw3 = jnp.concatenate([rw_hi, (rw_t - rw_hi.astype(F32)).astype(BF16)], axis=0)
    x1, aff_t = _merge(
        x, y_attn, conv, o_f, o_b, gr, mg,
        p["w_branch_attn"].astype(BF16), p["w_branch_conv"].astype(BF16), p["w_branch_gla"].astype(BF16),
        p["w_out"].astype(BF16), p["conv_w"].astype(F32), p["gla_norm_g"].astype(F32).reshape(1, -1),
        p["ln_mix_g"].astype(F32).reshape(1, -1), p["ln_mix_b"].astype(F32).reshape(1, -1), rw3)

    cap = CAPACITY_FACTOR * S // N_EXPERTS
    idx, gate, offs = _topk(aff_t, cap)
    y = _expert_ffn(x1, idx, gate.reshape(N_EXPERTS, cap, 1), experts["expert_w_gate"], experts["expert_w_up"],
                    experts["expert_w_down"], layer)
    rows_per_pass = S // COMBINE_PASSES // LANES
    starts = jnp.concatenate([offs[:, ::rows_per_pass], jnp.full((N_EXPERTS, 1), cap, I32)], axis=1)
    ffn = _combine(y.reshape(N_EXPERTS * cap, SUBLANES, LANES), idx, starts, S).reshape(S, D)
    return _residual_ln(x1, ffn, p["ln_ffn_g"].astype(F32).reshape(1, -1), p["ln_ffn_b"].astype(F32).reshape(1, -1))


def kernel(x, w_in, attn_sink, conv_w, gla_gate_w2, gla_gate_b, gla_norm_g, w_branch_attn, w_branch_conv,
           w_branch_gla, w_out, ln_mix_g, ln_mix_b, router_w, expert_w_gate, expert_w_up, expert_w_down,
           ln_ffn_g, ln_ffn_b):
    params = dict(w_in=w_in, attn_sink=attn_sink, conv_w=conv_w, gla_gate_w2=gla_gate_w2, gla_gate_b=gla_gate_b,
                  gla_norm_g=gla_norm_g, w_branch_attn=w_branch_attn, w_branch_conv=w_branch_conv,
                  w_branch_gla=w_branch_gla, w_out=w_out, ln_mix_g=ln_mix_g, ln_mix_b=ln_mix_b,
                  router_w=router_w, expert_w_gate=expert_w_gate, expert_w_up=expert_w_up,
                  expert_w_down=expert_w_down, ln_ffn_g=ln_ffn_g, ln_ffn_b=ln_ffn_b)
    B, S, D = x.shape
    outs = []
    for bi in range(B):
        h = x[bi]
        for l in range(DEPTH):
            h = _layer(h, {name: val[l] for name, val in params.items() if not name.startswith("expert_")},
                       params, l)
        outs.append(h)
    return jnp.stack(outs)
```

```python
import functools
import math

import numpy as np
import jax
import jax.numpy as jnp
from jax import lax
from jax.experimental import pallas as pl
from jax.experimental.pallas import tpu as pltpu

F32 = jnp.float32
BF16 = jnp.bfloat16
I32 = jnp.int32

D_MODEL = 1024
DEPTH = 2
ATT_HEADS, ATT_KV_HEADS, ATT_HEAD_DIM, ATT_BLOCK, ATT_WINDOW = 8, 2, 64, 128, 128
ATT_GROUP = ATT_HEADS // ATT_KV_HEADS
CONV_WIDTH = 512
GLA_HEADS, GLA_KEY_DIM, GLA_VALUE_DIM, GLA_GATE_RANK, GLA_TAU = 4, 256, 512, 16, 16.0
GLA_DK = GLA_KEY_DIM // GLA_HEADS
GLA_DV = GLA_VALUE_DIM // GLA_HEADS
N_EXPERTS, EXPERT_FF, CAPACITY_FACTOR = 16, 2048, 2
DN_ALPHA = (2 * DEPTH) ** 0.25
LN_EPS = 1e-5
RMS_EPS = 1e-6
ATT_Q, ATT_KV = ATT_HEADS * ATT_HEAD_DIM, ATT_KV_HEADS * ATT_HEAD_DIM

LANES = 128
SUBLANES = 8
NEG_BIG = -1e30

GLA_CHUNK = 128
VMEM_LIMIT = 56 * 1024 * 1024


def _cparams(sem, vmem=VMEM_LIMIT):
    return pltpu.CompilerParams(dimension_semantics=sem, vmem_limit_bytes=vmem)


def _proj_body(n_out, x_ref, *refs):
    xb = x_ref[...].astype(BF16)
    for w_ref, o_ref in zip(refs[:n_out], refs[n_out:]):
        o_ref[...] = jnp.dot(xb, w_ref[...], preferred_element_type=F32).astype(o_ref.dtype)


def _proj(x, ws, out_dtypes, tm, name):
    S, D = x.shape
    in_specs = [pl.BlockSpec((tm, D), lambda i: (i, 0))]
    in_specs += [pl.BlockSpec(w.shape, lambda i: (0, 0)) for w in ws]
    out_specs = [pl.BlockSpec((tm, w.shape[1]), lambda i: (i, 0)) for w in ws]
    out_shape = [jax.ShapeDtypeStruct((S, w.shape[1]), dt) for w, dt in zip(ws, out_dtypes)]
    return pl.pallas_call(
        functools.partial(_proj_body, len(ws)), grid=(S // tm,), in_specs=in_specs,
        out_specs=out_specs, out_shape=out_shape, name=name,
        compiler_params=_cparams(("parallel",)))(x, *ws)


ATT_TQ = 4 * ATT_BLOCK


def _attn_bias_table():
    q_off = np.arange(ATT_BLOCK)[:, None]
    k_off = np.arange(3 * ATT_BLOCK)[None, :] - ATT_BLOCK
    dist = np.abs(q_off - k_off).astype(np.float32)
    slopes = 2.0 ** (-8.0 * np.arange(1, ATT_HEADS + 1, dtype=np.float32) / ATT_HEADS)
    return np.stack([np.where(dist <= ATT_WINDOW, -sl * dist, NEG_BIG) for sl in slopes]).astype(np.float32)


def _attn_body(n_steps, sink_ref, cur_ref, prev_ref, next_ref, bias_ref, o_ref):
    i = pl.program_id(0)
    hd = ATT_HEAD_DIM
    per = ATT_TQ // ATT_BLOCK
    kv_all = jnp.concatenate(
        [prev_ref[:, ATT_Q:], cur_ref[:, ATT_Q:], next_ref[:, ATT_Q:]], axis=0)
    col = lax.broadcasted_iota(I32, (1, 3 * ATT_BLOCK), 1)
    for b in range(per):
        rows = slice(b * ATT_BLOCK, (b + 1) * ATT_BLOCK)
        kvb = kv_all[b * ATT_BLOCK:(b + 3) * ATT_BLOCK]
        edge = None
        if b == 0:
            edge = jnp.where(jnp.logical_and(col < ATT_BLOCK, i == 0), NEG_BIG, 0.0).astype(F32)
        if b == per - 1:
            last = jnp.where(jnp.logical_and(col >= 2 * ATT_BLOCK, i == n_steps - 1), NEG_BIG, 0.0).astype(F32)
            edge = last if edge is None else edge + last
        scores = []
        for h in range(ATT_HEADS):
            j = h // ATT_GROUP
            q = cur_ref[rows, h * hd:(h + 1) * hd]
            kj = kvb[:, j * hd:(j + 1) * hd]
            s = lax.dot_general(q, kj, (((1,), (1,)), ((), ())), preferred_element_type=F32) + bias_ref[h]
            scores.append(s if edge is None else s + edge)
        probs, scales = [], []
        for h in range(ATT_HEADS):
            sink = sink_ref[h]
            m = jnp.maximum(jnp.max(scores[h], axis=-1, keepdims=True), sink)
            p = jnp.exp(scores[h] - m)
            denom = jnp.sum(p, axis=-1, keepdims=True) + jnp.exp(sink - m)
            probs.append(p.astype(BF16))
            scales.append(1.0 / denom)
        outs = []
        for h in range(ATT_HEADS):
            j = h // ATT_GROUP
            vj = kvb[:, ATT_KV + j * hd:ATT_KV + (j + 1) * hd]
            outs.append(jnp.dot(probs[h], vj, preferred_element_type=F32) * scales[h])
        o_ref[rows, :] = jnp.concatenate(outs, axis=-1).astype(o_ref.dtype)


def _attention(qkv, sink):
    S = qkv.shape[0]
    n_blocks = S // ATT_BLOCK
    per = ATT_TQ // ATT_BLOCK
    n_steps = S // ATT_TQ
    width = qkv.shape[1]
    bias = jnp.asarray(_attn_bias_table())
    grid_spec = pltpu.PrefetchScalarGridSpec(
        num_scalar_prefetch=1, grid=(n_steps,),
        in_specs=[
            pl.BlockSpec((ATT_TQ, width), lambda i, sk: (i, 0)),
            pl.BlockSpec((ATT_BLOCK, width), lambda i, sk: (jnp.maximum(i * per - 1, 0), 0)),
            pl.BlockSpec((ATT_BLOCK, width), lambda i, sk: (jnp.minimum((i + 1) * per, n_blocks - 1), 0)),
            pl.BlockSpec(bias.shape, lambda i, sk: (0, 0, 0)),
        ],
        out_specs=pl.BlockSpec((ATT_TQ, ATT_Q), lambda i, sk: (i, 0)))
    return pl.pallas_call(
        functools.partial(_attn_body, n_steps), grid_spec=grid_spec,
        out_shape=jax.ShapeDtypeStruct((S, ATT_Q), BF16), name="attn",
        compiler_params=_cparams(("parallel",)))(sink.astype(F32), qkv, qkv, qkv, bias)


GLA_TB = 512


def _mid_broadcast(x, w, pos, reverse):
    C, N = x.shape
    if 2 * w >= SUBLANES:
        x3 = x.reshape(C // (2 * w), 2 * w, N)
        r = w if reverse else w - 1
        return jnp.broadcast_to(x3[:, r:r + 1, :], x3.shape).reshape(C, N)
    out = jnp.zeros_like(x)
    for o in range(-(w - 1), w + 1):
        shift = (-o if reverse else o) % C
        src = x if shift == 0 else pltpu.roll(x, shift, 0)
        out = jnp.where(pos - (w - 1) == o, src, out)
    return out


def _gla_body(reverse, qk_ref, v_ref, lr_ref, w2_ref, gb_ref, o_ref, state_ref):
    C = GLA_CHUNK
    H, dk, dv = GLA_HEADS, GLA_DK, GLA_DV

    @pl.when(pl.program_id(0) == 0)
    def _():
        state_ref[...] = jnp.zeros_like(state_ref)

    row = lax.broadcasted_iota(I32, (C, 1), 0)
    rowi = lax.broadcasted_iota(I32, (C, C), 0)
    coli = lax.broadcasted_iota(I32, (C, C), 1)
    eye = rowi == coli

    def scan_pos(idx, w):
        p = idx & (2 * w - 1)
        return (2 * w - 1) - p if reverse else p

    levels = []
    w = 1
    while w < C:
        shift = int(math.log2(2 * w))
        pair = jnp.logical_and(jnp.right_shift(rowi, shift) == jnp.right_shift(coli, shift),
                               jnp.logical_and(scan_pos(rowi, w) >= w, scan_pos(coli, w) < w))
        pos = scan_pos(row, w)
        levels.append((w, pos, pos >= w, jnp.where(pair, 1.0, 0.0).astype(F32)))
        w *= 2

    def prepare(c):
        rows = slice(c * C, (c + 1) * C)
        q = qk_ref[rows, :GLA_KEY_DIM] * (dk ** -0.5)
        k = qk_ref[rows, GLA_KEY_DIM:]
        z = jnp.dot(lr_ref[rows, :].astype(BF16), w2_ref[...], preferred_element_type=F32) + gb_ref[...]
        la = (jnp.minimum(z, 0.0) - jnp.log(1.0 + jnp.exp(-jnp.abs(z)))) * (1.0 / GLA_TAU)
        cw = la
        scores = [None] * H
        for w, pos, second, pair in levels:
            mid = _mid_broadcast(cw, w, pos, reverse)
            fac = jnp.exp(jnp.where(second, cw, mid - cw))
            t = (jnp.where(second, q, k) * fac).astype(BF16)
            for h in range(H):
                th = t[:, h * dk:(h + 1) * dk]
                sc = lax.dot_general(th, th, (((1,), (1,)), ((), ())), preferred_element_type=F32)
                sc = sc * pair
                scores[h] = sc if scores[h] is None else scores[h] + sc
            cw = cw + jnp.where(second, mid, 0.0)
        b = cw
        btot = b[0:1, :] if reverse else b[C - 1:C, :]
        qhat = (q * jnp.exp(b)).astype(BF16)
        khat = (k * jnp.exp(btot - b)).astype(BF16)
        qk_diag = q * k
        a_mats = []
        for h in range(H):
            diag = jnp.sum(qk_diag[:, h * dk:(h + 1) * dk], axis=-1, keepdims=True)
            a_mats.append((scores[h] + jnp.where(eye, diag, 0.0)).astype(BF16))
        return rows, a_mats, qhat, khat, jnp.exp(btot)

    def finish(prep):
        rows, a_mats, qhat, khat, a_tot = prep
        v = v_ref[rows, :]
        for h in range(H):
            ksl = slice(h * dk, (h + 1) * dk)
            vsl = slice(h * dv, (h + 1) * dv)
            vh = v[:, vsl]
            st = state_ref[h]
            o_h = jnp.dot(a_mats[h], vh.astype(BF16), preferred_element_type=F32)
            o_h = o_h + lax.dot_general(qhat[:, ksl], st.astype(BF16), (((1,), (1,)), ((), ())),
                                        preferred_element_type=F32)
            o_ref[rows, vsl] = o_h
            upd = jnp.dot(vh.T.astype(BF16), khat[:, ksl], preferred_element_type=F32)
            state_ref[h] = st * a_tot[:, ksl] + upd

    n_chunks = GLA_TB // C
    order = list(range(n_chunks - 1, -1, -1) if reverse else range(n_chunks))
    prep = prepare(order[0])
    for nxt in order[1:]:
        prep_next = prepare(nxt)
        finish(prep)
        prep = prep_next
    finish(prep)


def _gla_direction(qk, v, lr, w2p, gb, reverse):
    S = qk.shape[0]
    n = S // GLA_TB
    blk = (lambda i: (n - 1 - i, 0)) if reverse else (lambda i: (i, 0))
    return pl.pallas_call(
        functools.partial(_gla_body, reverse), grid=(n,),
        in_specs=[
            pl.BlockSpec((GLA_TB, 2 * GLA_KEY_DIM), blk),
            pl.BlockSpec((GLA_TB, GLA_VALUE_DIM), blk),
            pl.BlockSpec((GLA_TB, LANES), blk),
            pl.BlockSpec(w2p.shape, lambda i: (0, 0)),
            pl.BlockSpec(gb.shape, lambda i: (0, 0)),
        ],
        out_specs=pl.BlockSpec((GLA_TB, GLA_VALUE_DIM), blk),
        out_shape=jax.ShapeDtypeStruct((S, GLA_VALUE_DIM), F32),
        scratch_shapes=[pltpu.VMEM((GLA_HEADS, GLA_DV, GLA_DK), F32)], name="gla",
        compiler_params=_cparams(("arbitrary",)))(qk, v, lr, w2p, gb)


MERGE_TM = 512


def _layer_norm(h, g, b):
    mu = jnp.mean(h, axis=-1, keepdims=True)
    d = h - mu
    var = jnp.mean(d * d, axis=-1, keepdims=True)
    return d * lax.rsqrt(var + LN_EPS) * g + b


def _sigmoid(x):
    return 0.5 * jnp.tanh(0.5 * x) + 0.5


def _split3(x):
    hi = x.astype(BF16)
    r1 = x - hi.astype(F32)
    mid = r1.astype(BF16)
    lo = (r1 - mid.astype(F32)).astype(BF16)
    return hi, mid, lo


def _merge_body(n_tiles, x_ref, ya_ref, cv_ref, cvp_ref, cvn_ref, of_ref, ob_ref, r_ref, mg_ref,
                wa_ref, wc_ref, wg_ref, wo_ref, cw_ref, ng_ref, lg_ref, lb_ref, rw_ref,
                x1_ref, aff_ref):
    i = pl.program_id(0)
    T = MERGE_TM
    CW = CONV_WIDTH
    u = cv_ref[:, 2 * CW:] * cv_ref[:, :CW]
    u_before = cvp_ref[SUBLANES - 1:SUBLANES, 2 * CW:] * cvp_ref[SUBLANES - 1:SUBLANES, :CW]
    u_after = cvn_ref[0:1, 2 * CW:] * cvn_ref[0:1, :CW]
    u_before = jnp.where(i == 0, 0.0, u_before)
    u_after = jnp.where(i == n_tiles - 1, 0.0, u_after)
    row = lax.broadcasted_iota(I32, (T, 1), 0)
    u_m1 = jnp.where(row == 0, u_before, pltpu.roll(u, 1, 0))
    u_p1 = jnp.where(row == T - 1, u_after, pltpu.roll(u, T - 1, 0))
    y_conv = cv_ref[:, CW:2 * CW] * (cw_ref[0:1, :] * u_m1 + cw_ref[1:2, :] * u + cw_ref[2:3, :] * u_p1)

    o = of_ref[...] + ob_ref[...]
    parts = []
    for h in range(GLA_HEADS):
        oh = o[:, h * GLA_DV:(h + 1) * GLA_DV]
        parts.append(oh * lax.rsqrt(jnp.mean(oh * oh, axis=-1, keepdims=True) + RMS_EPS))
    r = r_ref[...]
    y_gla = jnp.concatenate(parts, axis=-1) * ng_ref[...] * (r * _sigmoid(r))

    D = D_MODEL
    merged = _sigmoid(mg_ref[:, :D]) * jnp.dot(ya_ref[...], wa_ref[...], preferred_element_type=F32)
    merged += _sigmoid(mg_ref[:, D:2 * D]) * jnp.dot(y_conv.astype(BF16), wc_ref[...], preferred_element_type=F32)
    merged += _sigmoid(mg_ref[:, 2 * D:]) * jnp.dot(y_gla.astype(BF16), wg_ref[...], preferred_element_type=F32)
    hres = DN_ALPHA * x_ref[...] + jnp.dot(merged.astype(BF16), wo_ref[...], preferred_element_type=F32)
    x1 = _layer_norm(hres, lg_ref[...], lb_ref[...])
    x1_ref[...] = x1

    nt = (((1,), (1,)), ((), ()))
    xh = x1.astype(BF16)
    xl = (x1 - xh.astype(F32)).astype(BF16)
    both = lax.dot_general(rw_ref[...], xh, nt, preferred_element_type=F32)
    logits = (both[:N_EXPERTS] + both[N_EXPERTS:]
              + lax.dot_general(rw_ref[:N_EXPERTS], xl, nt, preferred_element_type=F32))
    mx = jnp.max(logits, axis=0, keepdims=True)
    ex = jnp.exp(logits - mx)
    aff_ref[...] = ex / jnp.sum(ex, axis=0, keepdims=True)


def _merge(x, y_attn, conv, o_f, o_b, r, mg, wa, wc, wg, wo, cw, ng, lg, lb, rw3):
    S, D = x.shape
    T = MERGE_TM
    n_tiles = S // T
    per = T // SUBLANES
    n8 = S // SUBLANES
    row_spec = lambda wd: pl.BlockSpec((T, wd), lambda i: (i, 0))
    full = lambda a: pl.BlockSpec(a.shape, lambda i: (0,) * a.ndim)
    return pl.pallas_call(
        functools.partial(_merge_body, n_tiles), grid=(n_tiles,),
        in_specs=[
            row_spec(D), row_spec(ATT_Q), row_spec(3 * CONV_WIDTH),
            pl.BlockSpec((SUBLANES, 3 * CONV_WIDTH), lambda i: (jnp.maximum(i * per - 1, 0), 0)),
            pl.BlockSpec((SUBLANES, 3 * CONV_WIDTH), lambda i: (jnp.minimum((i + 1) * per, n8 - 1), 0)),
            row_spec(GLA_VALUE_DIM), row_spec(GLA_VALUE_DIM), row_spec(GLA_VALUE_DIM), row_spec(3 * D),
            full(wa), full(wc), full(wg), full(wo), full(cw), full(ng), full(lg), full(lb), full(rw3),
        ],
        out_specs=[row_spec(D), pl.BlockSpec((N_EXPERTS, T), lambda i: (0, i))],
        out_shape=[jax.ShapeDtypeStruct((S, D), F32), jax.ShapeDtypeStruct((N_EXPERTS, S), F32)],
        name="merge", compiler_params=_cparams(("parallel",)))(
            x, y_attn, conv, conv, conv, o_f, o_b, r, mg, wa, wc, wg, wo, cw, ng, lg, lb, rw3)


def _thr_body(k, aff_ref, thr_ref):
    a = aff_ref[...]

    def step(t, thr):
        cand = thr | jnp.left_shift(jnp.int32(1), 30 - t)
        ge = (a >= pltpu.bitcast(cand, F32)).astype(I32)
        cnt = jnp.sum(jnp.sum(ge, axis=1, keepdims=True), axis=2, keepdims=True)
        return jnp.where(cnt >= k, cand, thr)

    thr = lax.fori_loop(0, 31, step, jnp.zeros((N_EXPERTS, 1, 1), I32))
    thr_ref[...] = jnp.broadcast_to(thr, thr_ref.shape)


def _select_body(k, aff_ref, thr_ref, idx_ref, gate_ref, offs_ref):
    R = aff_ref.shape[0]
    L = LANES
    a = aff_ref[...]
    thr_bits = thr_ref[...]
    thr = pltpu.bitcast(thr_bits, F32)[0:1, 0:1]
    thr_next = pltpu.bitcast(thr_bits + 1, F32)[0:1, 0:1]
    gt = a >= thr_next
    eq = jnp.logical_and(a >= thr, jnp.logical_not(gt))
    n_gt = jnp.sum(jnp.sum(gt.astype(F32), axis=0, keepdims=True), axis=1, keepdims=True)
    need = float(k) - n_gt

    li = lax.broadcasted_iota(I32, (L, L), 0)
    lj = lax.broadcasted_iota(I32, (L, L), 1)
    upper = jnp.where(li <= lj, 1.0, 0.0).astype(BF16)
    ri = lax.broadcasted_iota(I32, (R, R), 0)
    rj = lax.broadcasted_iota(I32, (R, R), 1)
    lower_strict = jnp.where(rj < ri, 1.0, 0.0).astype(BF16)

    def prefix(mask_f):
        cum = jnp.dot(mask_f.astype(BF16), upper, preferred_element_type=F32)
        tot = jnp.broadcast_to(cum[:, L - 1:L], (R, L))
        offs = jnp.dot(lower_strict, tot.astype(BF16), preferred_element_type=F32)
        return cum, tot, offs

    eq_f = eq.astype(F32)
    cum_eq, _, offs_eq = prefix(eq_f)
    rank_eq = cum_eq - eq_f + offs_eq
    sel = jnp.logical_or(gt, jnp.logical_and(eq, rank_eq < need))
    sel_f = sel.astype(F32)
    cum, tot, offs = prefix(sel_f)
    offs_ref[...] = offs.astype(I32)

    offs_incl = offs + tot
    cum_t = cum.T.astype(BF16)
    a_t = a.T
    ah, am, al = _split3(a_t)
    row_iota = lax.broadcasted_iota(I32, (R, L), 0).astype(F32)
    lane_iota_col = lax.broadcasted_iota(I32, (L, L), 0).astype(F32)
    slot_lane = lax.broadcasted_iota(I32, (1, L), 1).astype(F32)

    def chunk(ci, carry):
        slot = slot_lane + lax.convert_element_type(ci * L, F32)
        row_of = jnp.sum((offs_incl <= slot).astype(F32), axis=0, keepdims=True)
        onehot = (row_iota == row_of).astype(F32)
        base = jnp.sum(onehot * offs, axis=0, keepdims=True)
        local = slot - base
        oh = onehot.astype(BF16)
        cum_at = jnp.dot(cum_t, oh, preferred_element_type=F32)
        lane_of = jnp.sum((cum_at <= local).astype(F32), axis=0, keepdims=True)
        val = (jnp.dot(ah, oh, preferred_element_type=F32) + jnp.dot(am, oh, preferred_element_type=F32)
               + jnp.dot(al, oh, preferred_element_type=F32))
        gate = jnp.sum(jnp.where(lane_iota_col == lane_of, val, 0.0), axis=0, keepdims=True)
        idx_ref[pl.ds(ci, 1), :] = (row_of * L + lane_of).astype(I32)
        gate_ref[pl.ds(ci, 1), :] = gate
        return carry

    lax.fori_loop(0, idx_ref.shape[0], chunk, 0)


def _topk(aff_t, k):
    E, S = aff_t.shape
    R = S // LANES
    aff3 = aff_t.reshape(E, R, LANES)
    thr = pl.pallas_call(
        functools.partial(_thr_body, k), grid=(1,),
        in_specs=[pl.BlockSpec((E, R, LANES), lambda i: (0, 0, 0))],
        out_specs=pl.BlockSpec((E, SUBLANES, LANES), lambda i: (0, 0, 0)),
        out_shape=jax.ShapeDtypeStruct((E, SUBLANES, LANES), I32), name="topk_thr",
        compiler_params=_cparams(("arbitrary",)))(aff3)
    n_chunks = k // LANES
    idx, gate, offs = pl.pallas_call(
        functools.partial(_select_body, k), grid=(E,),
        in_specs=[pl.BlockSpec((None, R, LANES), lambda e: (e, 0, 0)),
                  pl.BlockSpec((None, SUBLANES, LANES), lambda e: (e, 0, 0))],
        out_specs=[pl.BlockSpec((None, n_chunks, LANES), lambda e: (e, 0, 0)),
                   pl.BlockSpec((None, n_chunks, LANES), lambda e: (e, 0, 0)),
                   pl.BlockSpec((None, R, LANES), lambda e: (e, 0, 0))],
        out_shape=[jax.ShapeDtypeStruct((E, n_chunks, LANES), I32),
                   jax.ShapeDtypeStruct((E, n_chunks, LANES), F32),
                   jax.ShapeDtypeStruct((E, R, LANES), I32)],
        name="topk_select", compiler_params=_cparams(("parallel",)))(aff3, thr)
    return idx.reshape(E, k), gate.reshape(E, k), offs[:, :, 0]


FFN_TM = 1024
FFN_TF = 512


def _ffn_body(tm, n_f, idx_ref, idx_next_ref, x_hbm, gate_ref, wg_ref, wu_ref, wd_ref, y_ref, xs_ref, xb_ref,
              sem):
    f = pl.program_id(2)
    n_tiles = pl.num_programs(0) * pl.num_programs(1)
    tile = pl.program_id(0) * pl.num_programs(1) + pl.program_id(1)
    slot = tile % 2
    per_step = tm // n_f

    def row_copy(ids, j, s):
        return pltpu.make_async_copy(x_hbm.at[pl.ds(ids[0, j], 1), :], xs_ref.at[s, pl.ds(j, 1), :], sem.at[s])

    def wait_tile(s):
        pltpu.make_async_copy(x_hbm.at[pl.ds(0, tm), :], xs_ref.at[s], sem.at[s]).wait()

    @pl.when(f == 0)
    def _():
        @pl.when(tile == 0)
        def _():
            def body(j, c):
                row_copy(idx_ref, j, slot).start()
                return c
            lax.fori_loop(0, tm, body, 0)

        wait_tile(slot)
        xb_ref[...] = xs_ref[slot].astype(BF16)

    for j in range(per_step):
        row_copy(idx_next_ref, f * per_step + j, 1 - slot).start()

    xb = xb_ref[...]
    hg = jnp.dot(xb, wg_ref[...].astype(BF16), preferred_element_type=F32)
    hu = jnp.dot(xb, wu_ref[...].astype(BF16), preferred_element_type=F32)
    hh = (hg * jax.nn.sigmoid(hg) * hu).astype(BF16)
    part = jnp.dot(hh, wd_ref[...].astype(BF16), preferred_element_type=F32)

    @pl.when(f == 0)
    def _():
        y_ref[...] = part

    @pl.when(f > 0)
    def _():
        y_ref[...] += part

    @pl.when(f == n_f - 1)
    def _():
        y_ref[...] = y_ref[...] * gate_ref[...]

        @pl.when(tile == n_tiles - 1)
        def _():
            wait_tile(1 - slot)


def _expert_ffn(x, idx, gate_col, w_gate, w_up, w_down, layer):
    E, cap = idx.shape
    D = x.shape[1]
    FF = w_gate.shape[-1]
    tm = min(FFN_TM, cap)
    mt = cap // tm
    n_tiles = E * mt
    idx3 = idx.reshape(n_tiles, 1, tm)
    smem_blk = lambda fn: pl.BlockSpec((None, 1, tm), fn, memory_space=pltpu.SMEM)
    return pl.pallas_call(
        functools.partial(_ffn_body, tm, FF // FFN_TF), grid=(E, mt, FF // FFN_TF),
        in_specs=[
            smem_blk(lambda e, m, f: (e * mt + m, 0, 0)),
            smem_blk(lambda e, m, f: (jnp.minimum(e * mt + m + 1, n_tiles - 1), 0, 0)),
            pl.BlockSpec(memory_space=pl.ANY),
            pl.BlockSpec((None, tm, 1), lambda e, m, f: (e, m, 0)),
            pl.BlockSpec((None, None, D, FFN_TF), lambda e, m, f: (layer, e, 0, f)),
            pl.BlockSpec((None, None, D, FFN_TF), lambda e, m, f: (layer, e, 0, f)),
            pl.BlockSpec((None, None, FFN_TF, D), lambda e, m, f: (layer, e, f, 0)),
        ],
        out_specs=pl.BlockSpec((None, tm, D), lambda e, m, f: (e, m, 0)),
        out_shape=jax.ShapeDtypeStruct((E, cap, D), F32),
        scratch_shapes=[pltpu.VMEM((2, tm, D), F32), pltpu.VMEM((tm, D), BF16), pltpu.SemaphoreType.DMA((2,))],
        name="ffn", compiler_params=_cparams(("arbitrary", "arbitrary", "arbitrary")))(
            idx3, idx3, x, gate_col, w_gate, w_up, w_down)


COMBINE_PASSES = 4
COMBINE_ROWS = 256
COMBINE_UNROLL = 8


def _combine_body(cap, q_tokens, n_rows_total, start_ref, idx_ref, y_ref, acc_ref, buf_ref, sem, cnt_ref):
    qi = pl.program_id(0)
    e = pl.program_id(1)
    n_q = pl.num_programs(0)
    n_e = pl.num_programs(1)
    first = jnp.logical_and(qi == 0, e == 0)

    @pl.when(first)
    def _():
        cnt_ref[0] = 0

    @pl.when(e == 0)
    def _():
        acc_ref[...] = jnp.zeros_like(acc_ref)

    def span(ee, qq):
        lo = start_ref[ee, qq]
        hi = start_ref[ee, qq + 1]
        return lo, hi, (hi - lo + COMBINE_ROWS - 1) // COMBINE_ROWS

    def chunk_src(ee, lo, ci):
        return jnp.minimum(ee * cap + lo + ci * COMBINE_ROWS, n_rows_total - COMBINE_ROWS)

    def copy(ee, lo, ci, slot):
        return pltpu.make_async_copy(y_ref.at[pl.ds(chunk_src(ee, lo, ci), COMBINE_ROWS)], buf_ref.at[slot],
                                     sem.at[slot])

    a, b, n_chunks = span(e, qi)
    tok0 = qi * q_tokens
    g0 = cnt_ref[0]

    @pl.when(jnp.logical_and(first, n_chunks > 0))
    def _():
        copy(e, a, 0, g0 % 2).start()

    def chunk(ci, carry):
        slot = (g0 + ci) % 2

        @pl.when(ci + 1 < n_chunks)
        def _():
            copy(e, a, ci + 1, 1 - slot).start()

        copy(e, a, ci, slot).wait()
        lo = a + ci * COMBINE_ROWS
        hi = jnp.minimum(lo + COMBINE_ROWS, b)
        shift = e * cap - chunk_src(e, a, ci)

        def add_rows(j0, n):
            ts = [idx_ref[0, j0 + u] - tok0 for u in range(n)]
            sums = [acc_ref[ts[u]] + buf_ref[slot, j0 + u + shift] for u in range(n)]
            for u in range(n):
                acc_ref[ts[u]] = sums[u]

        n_groups = (hi - lo) // COMBINE_UNROLL

        def add_group(g, c):
            add_rows(lo + g * COMBINE_UNROLL, COMBINE_UNROLL)
            return c

        def add_one(j, c):
            add_rows(j, 1)
            return c

        lax.fori_loop(0, n_groups, add_group, 0)
        lax.fori_loop(lo + n_groups * COMBINE_UNROLL, hi, add_one, 0)
        return carry

    lax.fori_loop(0, n_chunks, chunk, 0)
    g1 = g0 + n_chunks
    cnt_ref[0] = g1

    @pl.when(jnp.logical_not(jnp.logical_and(qi == n_q - 1, e == n_e - 1)))
    def _():
        wrap = e == n_e - 1
        e2 = jnp.where(wrap, 0, e + 1)
        q2 = jnp.where(wrap, qi + 1, qi)
        a2, _, n2 = span(e2, q2)

        @pl.when(n2 > 0)
        def _():
            copy(e2, a2, 0, g1 % 2).start()


def _combine(y_rows, idx, starts, S):
    E, cap = idx.shape
    q_tokens = S // COMBINE_PASSES
    n_rows_total = y_rows.shape[0]
    grid_spec = pltpu.PrefetchScalarGridSpec(
        num_scalar_prefetch=1, grid=(COMBINE_PASSES, E),
        in_specs=[pl.BlockSpec((None, 1, cap), lambda q, e, st: (e, 0, 0), memory_space=pltpu.SMEM),
                  pl.BlockSpec(memory_space=pl.ANY)],
        out_specs=pl.BlockSpec((q_tokens, SUBLANES, LANES), lambda q, e, st: (q, 0, 0)),
        scratch_shapes=[pltpu.VMEM((2, COMBINE_ROWS, SUBLANES, LANES), F32), pltpu.SemaphoreType.DMA((2,)),
                        pltpu.SMEM((1,), I32)])
    return pl.pallas_call(
        functools.partial(_combine_body, cap, q_tokens, n_rows_total), grid_spec=grid_spec,
        out_shape=jax.ShapeDtypeStruct((S, SUBLANES, LANES), F32), name="combine",
        compiler_params=_cparams(("arbitrary", "arbitrary")))(starts, idx.reshape(E, 1, cap), y_rows)


LN_TM = 512


def _ln_body(x_ref, f_ref, g_ref, b_ref, o_ref):
    o_ref[...] = _layer_norm(DN_ALPHA * x_ref[...] + f_ref[...], g_ref[...], b_ref[...])


def _residual_ln(x, f, g, b):
    S, D = x.shape
    row = pl.BlockSpec((LN_TM, D), lambda i: (i, 0))
    vec = pl.BlockSpec((1, D), lambda i: (0, 0))
    return pl.pallas_call(
        _ln_body, grid=(S // LN_TM,), in_specs=[row, row, vec, vec], out_specs=row,
        out_shape=jax.ShapeDtypeStruct((S, D), F32), name="ln",
        compiler_params=_cparams(("parallel",)))(x, f, g, b)


PROJ_TM = 512


def _pad_cols(w, n):
    return jnp.pad(w, ((0, 0), (0, n - w.shape[1])))


def _layer(x, p, experts, layer):
    S, D = x.shape
    w_in = p["w_in"].astype(BF16)
    c0 = ATT_Q + 2 * ATT_KV
    c1 = c0 + 3 * CONV_WIDTH
    c2 = c1 + 2 * GLA_KEY_DIM
    c3 = c2 + GLA_VALUE_DIM
    c4 = c3 + GLA_VALUE_DIM
    c5 = c4 + 2 * GLA_GATE_RANK
    w_qkv = jnp.concatenate([w_in[:, :ATT_Q] * (ATT_HEAD_DIM ** -0.5), w_in[:, ATT_Q:c0]], axis=1)
    ws = [w_qkv, w_in[:, c0:c1], w_in[:, c1:c2], w_in[:, c2:c3], w_in[:, c3:c4],
          _pad_cols(w_in[:, c4:c5], LANES)]
    qkv, conv, gqk, gv, gr, glr = _proj(x, ws, [BF16, F32, F32, F32, F32, F32], PROJ_TM, "proj_mix")
    (mg,) = _proj(x, [w_in[:, c5:]], [F32], PROJ_TM, "proj_gates")

    y_attn = _attention(qkv, p["attn_sink"])

    w2 = p["gla_gate_w2"].astype(BF16)
    rk = GLA_GATE_RANK
    w2_f = jnp.zeros((LANES, GLA_KEY_DIM), BF16).at[:rk].set(w2[0])
    w2_b = jnp.zeros((LANES, GLA_KEY_DIM), BF16).at[rk:2 * rk].set(w2[1])
    gb = p["gla_gate_b"].astype(F32)
    o_f = _gla_direction(gqk, gv, glr, w2_f, gb[0:1], False)
    o_b = _gla_direction(gqk, gv, glr, w2_b, gb[1:2], True)

    rw_t = p["router_w"].T.astype(F32)
    rw_hi = rw_t.astype(BF16)
    rw3 = jnp.concatenate([rw_hi, (rw_t - rw_hi.astype(F32)).astype(BF16)], axis=0)
    x1, aff_t = _merge(
        x, y_attn, conv, o_f, o_b, gr, mg,
        p["w_branch_attn"].astype(BF16), p["w_branch_conv"].astype(BF16), p["w_branch_gla"].astype(BF16),
        p["w_out"].astype(BF16), p["conv_w"].astype(F32), p["gla_norm_g"].astype(F32).reshape(1, -1),
        p["ln_mix_g"].astype(F32).reshape(1, -1), p["ln_mix_b"].astype(F32).reshape(1, -1), rw3)

    cap = CAPACITY_FACTOR * S // N_EXPERTS
    idx, gate, offs = _topk(aff_t, cap)
    y = _expert_ffn(x1, idx, gate.reshape(N_EXPERTS, cap, 1), experts["expert_w_gate"], experts["expert_w_up"],
                    experts["expert_w_down"], layer)
    rows_per_pass = S // COMBINE_PASSES // LANES
    starts = jnp.concatenate([offs[:, ::rows_per_pass], jnp.full((N_EXPERTS, 1), cap, I32)], axis=1)
    ffn = _combine(y.reshape(N_EXPERTS * cap, SUBLANES, LANES), idx, starts, S).reshape(S, D)
    return _residual_ln(x1, ffn, p["ln_ffn_g"].astype(F32).reshape(1, -1), p["ln_ffn_b"].astype(F32).reshape(1, -1))


def kernel(x, w_in, attn_sink, conv_w, gla_gate_w2, gla_gate_b, gla_norm_g, w_branch_attn, w_branch_conv,
           w_branch_gla, w_out, ln_mix_g, ln_mix_b, router_w, expert_w_gate, expert_w_up, expert_w_down,
           ln_ffn_g, ln_ffn_b):
    params = dict(w_in=w_in, attn_sink=attn_sink, conv_w=conv_w, gla_gate_w2=gla_gate_w2, gla_gate_b=gla_gate_b,
                  gla_norm_g=gla_norm_g, w_branch_attn=w_branch_attn, w_branch_conv=w_branch_conv,
                  w_branch_gla=w_branch_gla, w_out=w_out, ln_mix_g=ln_mix_g, ln_mix_b=ln_mix_b,
                  router_w=router_w, expert_w_gate=expert_w_gate, expert_w_up=expert_w_up,
                  expert_w_down=expert_w_down, ln_ffn_g=ln_ffn_g, ln_ffn_b=ln_ffn_b)
    B, S, D = x.shape
    outs = []
    for bi in range(B):
        h = x[bi]
        for l in range(DEPTH):
            h = _layer(h, {name: val[l] for name, val in params.items() if not name.startswith("expert_")},
                       params, l)
        outs.append(h)
    return jnp.stack(outs)
```

```python
import functools
import math

import numpy as np
import jax
import jax.numpy as jnp
from jax import lax
from jax.experimental import pallas as pl
from jax.experimental.pallas import tpu as pltpu

F32 = jnp.float32
BF16 = jnp.bfloat16
I32 = jnp.int32

D_MODEL = 1024
DEPTH = 2
ATT_HEADS, ATT_KV_HEADS, ATT_HEAD_DIM, ATT_BLOCK, ATT_WINDOW = 8, 2, 64, 128, 128
ATT_GROUP = ATT_HEADS // ATT_KV_HEADS
CONV_WIDTH = 512
GLA_HEADS, GLA_KEY_DIM, GLA_VALUE_DIM, GLA_GATE_RANK, GLA_TAU = 4, 256, 512, 16, 16.0
GLA_DK = GLA_KEY_DIM // GLA_HEADS
GLA_DV = GLA_VALUE_DIM // GLA_HEADS
N_EXPERTS, EXPERT_FF, CAPACITY_FACTOR = 16, 2048, 2
DN_ALPHA = (2 * DEPTH) ** 0.25
LN_EPS = 1e-5
RMS_EPS = 1e-6
ATT_Q, ATT_KV = ATT_HEADS * ATT_HEAD_DIM, ATT_KV_HEADS * ATT_HEAD_DIM

LANES = 128
SUBLANES = 8
NEG_BIG = -1e30

GLA_CHUNK = 128
VMEM_LIMIT = 56 * 1024 * 1024


def _cparams(sem, vmem=VMEM_LIMIT):
    return pltpu.CompilerParams(dimension_semantics=sem, vmem_limit_bytes=vmem)


def _proj_body(n_out, x_ref, *refs):
    xb = x_ref[...].astype(BF16)
    for w_ref, o_ref in zip(refs[:n_out], refs[n_out:]):
        o_ref[...] = jnp.dot(xb, w_ref[...], preferred_element_type=F32).astype(o_ref.dtype)


def _proj(x, ws, out_dtypes, tm, name):
    S, D = x.shape
    in_specs = [pl.BlockSpec((tm, D), lambda i: (i, 0))]
    in_specs += [pl.BlockSpec(w.shape, lambda i: (0, 0)) for w in ws]
    out_specs = [pl.BlockSpec((tm, w.shape[1]), lambda i: (i, 0)) for w in ws]
    out_shape = [jax.ShapeDtypeStruct((S, w.shape[1]), dt) for w, dt in zip(ws, out_dtypes)]
    return pl.pallas_call(
        functools.partial(_proj_body, len(ws)), grid=(S // tm,), in_specs=in_specs,
        out_specs=out_specs, out_shape=out_shape, name=name,
        compiler_params=_cparams(("parallel",)))(x, *ws)


ATT_TQ = 4 * ATT_BLOCK


def _attn_bias_table():
    q_off = np.arange(ATT_BLOCK)[:, None]
    k_off = np.arange(3 * ATT_BLOCK)[None, :] - ATT_BLOCK
    dist = np.abs(q_off - k_off).astype(np.float32)
    slopes = 2.0 ** (-8.0 * np.arange(1, ATT_HEADS + 1, dtype=np.float32) / ATT_HEADS)
    return np.stack([np.where(dist <= ATT_WINDOW, -sl * dist, NEG_BIG) for sl in slopes]).astype(np.float32)


def _attn_body(n_steps, sink_ref, cur_ref, prev_ref, next_ref, bias_ref, o_ref):
    i = pl.program_id(0)
    hd = ATT_HEAD_DIM
    per = ATT_TQ // ATT_BLOCK
    kv_all = jnp.concatenate(
        [prev_ref[:, ATT_Q:], cur_ref[:, ATT_Q:], next_ref[:, ATT_Q:]], axis=0)
    col = lax.broadcasted_iota(I32, (1, 3 * ATT_BLOCK), 1)
    for b in range(per):
        rows = slice(b * ATT_BLOCK, (b + 1) * ATT_BLOCK)
        kvb = kv_all[b * ATT_BLOCK:(b + 3) * ATT_BLOCK]
        edge = None
        if b == 0:
            edge = jnp.where(jnp.logical_and(col < ATT_BLOCK, i == 0), NEG_BIG, 0.0).astype(F32)
        if b == per - 1:
            last = jnp.where(jnp.logical_and(col >= 2 * ATT_BLOCK, i == n_steps - 1), NEG_BIG, 0.0).astype(F32)
            edge = last if edge is None else edge + last
        scores = []
        for h in range(ATT_HEADS):
            j = h // ATT_GROUP
            q = cur_ref[rows, h * hd:(h + 1) * hd]
            kj = kvb[:, j * hd:(j + 1) * hd]
            s = lax.dot_general(q, kj, (((1,), (1,)), ((), ())), preferred_element_type=F32) + bias_ref[h]
            scores.append(s if edge is None else s + edge)
        probs, scales = [], []
        for h in range(ATT_HEADS):
            sink = sink_ref[h]
            m = jnp.maximum(jnp.max(scores[h], axis=-1, keepdims=True), sink)
            p = jnp.exp(scores[h] - m)
            denom = jnp.sum(p, axis=-1, keepdims=True) + jnp.exp(sink - m)
            probs.append(p.astype(BF16))
            scales.append(1.0 / denom)
        outs = []
        for h in range(ATT_HEADS):
            j = h // ATT_GROUP
            vj = kvb[:, ATT_KV + j * hd:ATT_KV + (j + 1) * hd]
            outs.append(jnp.dot(probs[h], vj, preferred_element_type=F32) * scales[h])
        o_ref[rows, :] = jnp.concatenate(outs, axis=-1).astype(o_ref.dtype)


def _attention(qkv, sink):
    S = qkv.shape[0]
    n_blocks = S // ATT_BLOCK
    per = ATT_TQ // ATT_BLOCK
    n_steps = S // ATT_TQ
    width = qkv.shape[1]
    bias = jnp.asarray(_attn_bias_table())
    grid_spec = pltpu.PrefetchScalarGridSpec(
        num_scalar_prefetch=1, grid=(n_steps,),
        in_specs=[
            pl.BlockSpec((ATT_TQ, width), lambda i, sk: (i, 0)),
            pl.BlockSpec((ATT_BLOCK, width), lambda i, sk: (jnp.maximum(i * per - 1, 0), 0)),
            pl.BlockSpec((ATT_BLOCK, width), lambda i, sk: (jnp.minimum((i + 1) * per, n_blocks - 1), 0)),
            pl.BlockSpec(bias.shape, lambda i, sk: (0, 0, 0)),
        ],
        out_specs=pl.BlockSpec((ATT_TQ, ATT_Q), lambda i, sk: (i, 0)))
    return pl.pallas_call(
        functools.partial(_attn_body, n_steps), grid_spec=grid_spec,
        out_shape=jax.ShapeDtypeStruct((S, ATT_Q), BF16), name="attn",
        compiler_params=_cparams(("parallel",)))(sink.astype(F32), qkv, qkv, qkv, bias)


GLA_TB = 512


def _mid_broadcast(x, w, pos, reverse):
    C, N = x.shape
    if 2 * w >= SUBLANES:
        x3 = x.reshape(C // (2 * w), 2 * w, N)
        r = w if reverse else w - 1
        return jnp.broadcast_to(x3[:, r:r + 1, :], x3.shape).reshape(C, N)
    out = jnp.zeros_like(x)
    for o in range(-(w - 1), w + 1):
        shift = (-o if reverse else o) % C
        src = x if shift == 0 else pltpu.roll(x, shift, 0)
        out = jnp.where(pos - (w - 1) == o, src, out)
    return out


def _gla_body(reverse, qk_ref, v_ref, lr_ref, w2_ref, gb_ref, o_ref, state_ref):
    C = GLA_CHUNK
    H, dk, dv = GLA_HEADS, GLA_DK, GLA_DV

    @pl.when(pl.program_id(0) == 0)
    def _():
        state_ref[...] = jnp.zeros_like(state_ref)

    row = lax.broadcasted_iota(I32, (C, 1), 0)
    rowi = lax.broadcasted_iota(I32, (C, C), 0)
    coli = lax.broadcasted_iota(I32, (C, C), 1)
    eye = rowi == coli

    def scan_pos(idx, w):
        p = idx & (2 * w - 1)
        return (2 * w - 1) - p if reverse else p

    levels = []
    w = 1
    while w < C:
        shift = int(math.log2(2 * w))
        pair = jnp.logical_and(jnp.right_shift(rowi, shift) == jnp.right_shift(coli, shift),
                               jnp.logical_and(scan_pos(rowi, w) >= w, scan_pos(coli, w) < w))
        pos = scan_pos(row, w)
        levels.append((w, pos, pos >= w, jnp.where(pair, 1.0, 0.0).astype(F32)))
        w *= 2

    def prepare(c):
        rows = slice(c * C, (c + 1) * C)
        q = qk_ref[rows, :GLA_KEY_DIM] * (dk ** -0.5)
        k = qk_ref[rows, GLA_KEY_DIM:]
        z = jnp.dot(lr_ref[rows, :].astype(BF16), w2_ref[...], preferred_element_type=F32) + gb_ref[...]
        la = (jnp.minimum(z, 0.0) - jnp.log(1.0 + jnp.exp(-jnp.abs(z)))) * (1.0 / GLA_TAU)
        cw = la
        scores = [None] * H
        for w, pos, second, pair in levels:
            mid = _mid_broadcast(cw, w, pos, reverse)
            fac = jnp.exp(jnp.where(second, cw, mid - cw))
            t = (jnp.where(second, q, k) * fac).astype(BF16)
            for h in range(H):
                th = t[:, h * dk:(h + 1) * dk]
                sc = lax.dot_general(th, th, (((1,), (1,)), ((), ())), preferred_element_type=F32)
                sc = sc * pair
                scores[h] = sc if scores[h] is None else scores[h] + sc
            cw = cw + jnp.where(second, mid, 0.0)
        b = cw
        btot = b[0:1, :] if reverse else b[C - 1:C, :]
        qhat = (q * jnp.exp(b)).astype(BF16)
        khat = (k * jnp.exp(btot - b)).astype(BF16)
        qk_diag = q * k
        a_mats = []
        for h in range(H):
            diag = jnp.sum(qk_diag[:, h * dk:(h + 1) * dk], axis=-1, keepdims=True)
            a_mats.append((scores[h] + jnp.where(eye, diag, 0.0)).astype(BF16))
        return rows, a_mats, qhat, khat, jnp.exp(btot)

    def finish(prep):
        rows, a_mats, qhat, khat, a_tot = prep
        v = v_ref[rows, :]
        for h in range(H):
            ksl = slice(h * dk, (h + 1) * dk)
            vsl = slice(h * dv, (h + 1) * dv)
            vh = v[:, vsl]
            st = state_ref[h]
            o_h = jnp.dot(a_mats[h], vh.astype(BF16), preferred_element_type=F32)
            o_h = o_h + lax.dot_general(qhat[:, ksl], st.astype(BF16), (((1,), (1,)), ((), ())),
                                        preferred_element_type=F32)
            o_ref[rows, vsl] = o_h
            upd = jnp.dot(vh.T.astype(BF16), khat[:, ksl], preferred_element_type=F32)
            state_ref[h] = st * a_tot[:, ksl] + upd

    n_chunks = GLA_TB // C
    order = list(range(n_chunks - 1, -1, -1) if reverse else range(n_chunks))
    prep = prepare(order[0])
    for nxt in order[1:]:
        prep_next = prepare(nxt)
        finish(prep)
        prep = prep_next
    finish(prep)


def _gla_direction(qk, v, lr, w2p, gb, reverse):
    S = qk.shape[0]
    n = S // GLA_TB
    blk = (lambda i: (n - 1 - i, 0)) if reverse else (lambda i: (i, 0))
    return pl.pallas_call(
        functools.partial(_gla_body, reverse), grid=(n,),
        in_specs=[
            pl.BlockSpec((GLA_TB, 2 * GLA_KEY_DIM), blk),
            pl.BlockSpec((GLA_TB, GLA_VALUE_DIM), blk),
            pl.BlockSpec((GLA_TB, LANES), blk),
            pl.BlockSpec(w2p.shape, lambda i: (0, 0)),
            pl.BlockSpec(gb.shape, lambda i: (0, 0)),
        ],
        out_specs=pl.BlockSpec((GLA_TB, GLA_VALUE_DIM), blk),
        out_shape=jax.ShapeDtypeStruct((S, GLA_VALUE_DIM), F32),
        scratch_shapes=[pltpu.VMEM((GLA_HEADS, GLA_DV, GLA_DK), F32)], name="gla",
        compiler_params=_cparams(("arbitrary",)))(qk, v, lr, w2p, gb)


MERGE_TM = 512


def _layer_norm(h, g, b):
    mu = jnp.mean(h, axis=-1, keepdims=True)
    d = h - mu
    var = jnp.mean(d * d, axis=-1, keepdims=True)
    return d * lax.rsqrt(var + LN_EPS) * g + b


def _sigmoid(x):
    return 0.5 * jnp.tanh(0.5 * x) + 0.5


def _split3(x):
    hi = x.astype(BF16)
    r1 = x - hi.astype(F32)
    mid = r1.astype(BF16)
    lo = (r1 - mid.astype(F32)).astype(BF16)
    return hi, mid, lo


def _merge_body(n_tiles, x_ref, ya_ref, cv_ref, cvp_ref, cvn_ref, of_ref, ob_ref, r_ref, mg_ref,
                wa_ref, wc_ref, wg_ref, wo_ref, cw_ref, ng_ref, lg_ref, lb_ref, rw_ref,
                x1_ref, aff_ref):
    i = pl.program_id(0)
    T = MERGE_TM
    CW = CONV_WIDTH
    u = cv_ref[:, 2 * CW:] * cv_ref[:, :CW]
    u_before = cvp_ref[SUBLANES - 1:SUBLANES, 2 * CW:] * cvp_ref[SUBLANES - 1:SUBLANES, :CW]
    u_after = cvn_ref[0:1, 2 * CW:] * cvn_ref[0:1, :CW]
    u_before = jnp.where(i == 0, 0.0, u_before)
    u_after = jnp.where(i == n_tiles - 1, 0.0, u_after)
    row = lax.broadcasted_iota(I32, (T, 1), 0)
    u_m1 = jnp.where(row == 0, u_before, pltpu.roll(u, 1, 0))
    u_p1 = jnp.where(row == T - 1, u_after, pltpu.roll(u, T - 1, 0))
    y_conv = cv_ref[:, CW:2 * CW] * (cw_ref[0:1, :] * u_m1 + cw_ref[1:2, :] * u + cw_ref[2:3, :] * u_p1)

    o = of_ref[...] + ob_ref[...]
    parts = []
    for h in range(GLA_HEADS):
        oh = o[:, h * GLA_DV:(h + 1) * GLA_DV]
        parts.append(oh * lax.rsqrt(jnp.mean(oh * oh, axis=-1, keepdims=True) + RMS_EPS))
    r = r_ref[...]
    y_gla = jnp.concatenate(parts, axis=-1) * ng_ref[...] * (r * _sigmoid(r))

    D = D_MODEL
    merged = _sigmoid(mg_ref[:, :D]) * jnp.dot(ya_ref[...], wa_ref[...], preferred_element_type=F32)
    merged += _sigmoid(mg_ref[:, D:2 * D]) * jnp.dot(y_conv.astype(BF16), wc_ref[...], preferred_element_type=F32)
    merged += _sigmoid(mg_ref[:, 2 * D:]) * jnp.dot(y_gla.astype(BF16), wg_ref[...], preferred_element_type=F32)
    hres = DN_ALPHA * x_ref[...] + jnp.dot(merged.astype(BF16), wo_ref[...], preferred_element_type=F32)
    x1 = _layer_norm(hres, lg_ref[...], lb_ref[...])
    x1_ref[...] = x1

    nt = (((1,), (1,)), ((), ()))
    xh = x1.astype(BF16)
    xl = (x1 - xh.astype(F32)).astype(BF16)
    both = lax.dot_general(rw_ref[...], xh, nt, preferred_element_type=F32)
    logits = (both[:N_EXPERTS] + both[N_EXPERTS:]
              + lax.dot_general(rw_ref[:N_EXPERTS], xl, nt, preferred_element_type=F32))
    mx = jnp.max(logits, axis=0, keepdims=True)
    ex = jnp.exp(logits - mx)
    aff_ref[...] = ex / jnp.sum(ex, axis=0, keepdims=True)


def _merge(x, y_attn, conv, o_f, o_b, r, mg, wa, wc, wg, wo, cw, ng, lg, lb, rw3):
    S, D = x.shape
    T = MERGE_TM
    n_tiles = S // T
    per = T // SUBLANES
    n8 = S // SUBLANES
    row_spec = lambda wd: pl.BlockSpec((T, wd), lambda i: (i, 0))
    full = lambda a: pl.BlockSpec(a.shape, lambda i: (0,) * a.ndim)
    return pl.pallas_call(
        functools.partial(_merge_body, n_tiles), grid=(n_tiles,),
        in_specs=[
            row_spec(D), row_spec(ATT_Q), row_spec(3 * CONV_WIDTH),
            pl.BlockSpec((SUBLANES, 3 * CONV_WIDTH), lambda i: (jnp.maximum(i * per - 1, 0), 0)),
            pl.BlockSpec((SUBLANES, 3 * CONV_WIDTH), lambda i: (jnp.minimum((i + 1) * per, n8 - 1), 0)),
            row_spec(GLA_VALUE_DIM), row_spec(GLA_VALUE_DIM), row_spec(GLA_VALUE_DIM), row_spec(3 * D),
            full(wa), full(wc), full(wg), full(wo), full(cw), full(ng), full(lg), full(lb), full(rw3),
        ],
        out_specs=[row_spec(D), pl.BlockSpec((N_EXPERTS, T), lambda i: (0, i))],
        out_shape=[jax.ShapeDtypeStruct((S, D), F32), jax.ShapeDtypeStruct((N_EXPERTS, S), F32)],
        name="merge", compiler_params=_cparams(("parallel",)))(
            x, y_attn, conv, conv, conv, o_f, o_b, r, mg, wa, wc, wg, wo, cw, ng, lg, lb, rw3)


def _thr_body(k, aff_ref, thr_ref):
    a = aff_ref[...]

    def step(t, thr):
        cand = thr | jnp.left_shift(jnp.int32(1), 30 - t)
        ge = (a >= pltpu.bitcast(cand, F32)).astype(I32)
        cnt = jnp.sum(jnp.sum(ge, axis=1, keepdims=True), axis=2, keepdims=True)
        return jnp.where(cnt >= k, cand, thr)

    thr = lax.fori_loop(0, 31, step, jnp.zeros((N_EXPERTS, 1, 1), I32))
    thr_ref[...] = jnp.broadcast_to(thr, thr_ref.shape)


def _select_body(k, aff_ref, thr_ref, idx_ref, gate_ref, offs_ref):
    R = aff_ref.shape[0]
    L = LANES
    a = aff_ref[...]
    thr_bits = thr_ref[...]
    thr = pltpu.bitcast(thr_bits, F32)[0:1, 0:1]
    thr_next = pltpu.bitcast(thr_bits + 1, F32)[0:1, 0:1]
    gt = a >= thr_next
    eq = jnp.logical_and(a >= thr, jnp.logical_not(gt))
    n_gt = jnp.sum(jnp.sum(gt.astype(F32), axis=0, keepdims=True), axis=1, keepdims=True)
    need = float(k) - n_gt

    li = lax.broadcasted_iota(I32, (L, L), 0)
    lj = lax.broadcasted_iota(I32, (L, L), 1)
    upper = jnp.where(li <= lj, 1.0, 0.0).astype(BF16)
    ri = lax.broadcasted_iota(I32, (R, R), 0)
    rj = lax.broadcasted_iota(I32, (R, R), 1)
    lower_strict = jnp.where(rj < ri, 1.0, 0.0).astype(BF16)

    def prefix(mask_f):
        cum = jnp.dot(mask_f.astype(BF16), upper, preferred_element_type=F32)
        tot = jnp.broadcast_to(cum[:, L - 1:L], (R, L))
        offs = jnp.dot(lower_strict, tot.astype(BF16), preferred_element_type=F32)
        return cum, tot, offs

    eq_f = eq.astype(F32)
    cum_eq, _, offs_eq = prefix(eq_f)
    rank_eq = cum_eq - eq_f + offs_eq
    sel = jnp.logical_or(gt, jnp.logical_and(eq, rank_eq < need))
    sel_f = sel.astype(F32)
    cum, tot, offs = prefix(sel_f)
    offs_ref[...] = offs.astype(I32)

    offs_incl = offs + tot
    cum_t = cum.T.astype(BF16)
    a_t = a.T
    ah, am, al = _split3(a_t)
    row_iota = lax.broadcasted_iota(I32, (R, L), 0).astype(F32)
    lane_iota_col = lax.broadcasted_iota(I32, (L, L), 0).astype(F32)
    slot_lane = lax.broadcasted_iota(I32, (1, L), 1).astype(F32)

    def chunk(ci, carry):
        slot = slot_lane + lax.convert_element_type(ci * L, F32)
        row_of = jnp.sum((offs_incl <= slot).astype(F32), axis=0, keepdims=True)
        onehot = (row_iota == row_of).astype(F32)
        base = jnp.sum(onehot * offs, axis=0, keepdims=True)
        local = slot - base
        oh = onehot.astype(BF16)
        cum_at = jnp.dot(cum_t, oh, preferred_element_type=F32)
        lane_of = jnp.sum((cum_at <= local).astype(F32), axis=0, keepdims=True)
        val = (jnp.dot(ah, oh, preferred_element_type=F32) + jnp.dot(am, oh, preferred_element_type=F32)
               + jnp.dot(al, oh, preferred_element_type=F32))
        gate = jnp.sum(jnp.where(lane_iota_col == lane_of, val, 0.0), axis=0, keepdims=True)
        idx_ref[pl.ds(ci, 1), :] = (row_of * L + lane_of).astype(I32)
        gate_ref[pl.ds(ci, 1), :] = gate
        return carry

    lax.fori_loop(0, idx_ref.shape[0], chunk, 0)


def _topk(aff_t, k):
    E, S = aff_t.shape
    R = S // LANES
    aff3 = aff_t.reshape(E, R, LANES)
    thr = pl.pallas_call(
        functools.partial(_thr_body, k), grid=(1,),
        in_specs=[pl.BlockSpec((E, R, LANES), lambda i: (0, 0, 0))],
        out_specs=pl.BlockSpec((E, SUBLANES, LANES), lambda i: (0, 0, 0)),
        out_shape=jax.ShapeDtypeStruct((E, SUBLANES, LANES), I32), name="topk_thr",
        compiler_params=_cparams(("arbitrary",)))(aff3)
    n_chunks = k // LANES
    idx, gate, offs = pl.pallas_call(
        functools.partial(_select_body, k), grid=(E,),
        in_specs=[pl.BlockSpec((None, R, LANES), lambda e: (e, 0, 0)),
                  pl.BlockSpec((None, SUBLANES, LANES), lambda e: (e, 0, 0))],
        out_specs=[pl.BlockSpec((None, n_chunks, LANES), lambda e: (e, 0, 0)),
                   pl.BlockSpec((None, n_chunks, LANES), lambda e: (e, 0, 0)),
                   pl.BlockSpec((None, R, LANES), lambda e: (e, 0, 0))],
        out_shape=[jax.ShapeDtypeStruct((E, n_chunks, LANES), I32),
                   jax.ShapeDtypeStruct((E, n_chunks, LANES), F32),
                   jax.ShapeDtypeStruct((E, R, LANES), I32)],
        name="topk_select", compiler_params=_cparams(("parallel",)))(aff3, thr)
    return idx.reshape(E, k), gate.reshape(E, k), offs[:, :, 0]


FFN_TM = 1024
FFN_TF = 512


def _ffn_body(tm, n_f, idx_ref, idx_next_ref, x_hbm, gate_ref, wg_ref, wu_ref, wd_ref, y_ref, xs_ref, xb_ref,
              acc_ref, sem):
    f = pl.program_id(2)
    n_tiles = pl.num_programs(0) * pl.num_programs(1)
    tile = pl.program_id(0) * pl.num_programs(1) + pl.program_id(1)
    slot = tile % 2
    per_step = tm // n_f

    def row_copy(ids, j, s):
        return pltpu.make_async_copy(x_hbm.at[pl.ds(ids[0, j], 1), :], xs_ref.at[s, pl.ds(j, 1), :], sem.at[s])

    def wait_tile(s):
        pltpu.make_async_copy(x_hbm.at[pl.ds(0, tm), :], xs_ref.at[s], sem.at[s]).wait()

    @pl.when(f == 0)
    def _():
        @pl.when(tile == 0)
        def _():
            def body(j, c):
                row_copy(idx_ref, j, slot).start()
                return c
            lax.fori_loop(0, tm, body, 0)

        wait_tile(slot)
        xb_ref[...] = xs_ref[slot].astype(BF16)

    for j in range(per_step):
        row_copy(idx_next_ref, f * per_step + j, 1 - slot).start()

    xb = xb_ref[...]
    hg = jnp.dot(xb, wg_ref[...].astype(BF16), preferred_element_type=F32)
    hu = jnp.dot(xb, wu_ref[...].astype(BF16), preferred_element_type=F32)
    hh = (hg * jax.nn.sigmoid(hg) * hu).astype(BF16)
    part = jnp.dot(hh, wd_ref[...].astype(BF16), preferred_element_type=F32)

    @pl.when(f == 0)
    def _():
        acc_ref[...] = part

    @pl.when(f > 0)
    def _():
        acc_ref[...] += part

    @pl.when(f == n_f - 1)
    def _():
        y_ref[...] = pltpu.einshape("m(sl)->msl", acc_ref[...] * gate_ref[...], s=SUBLANES, l=LANES)

        @pl.when(tile == n_tiles - 1)
        def _():
            wait_tile(1 - slot)


def _expert_ffn(x, idx, gate_col, w_gate, w_up, w_down, layer):
    E, cap = idx.shape
    D = x.shape[1]
    FF = w_gate.shape[-1]
    tm = min(FFN_TM, cap)
    mt = cap // tm
    n_tiles = E * mt
    idx3 = idx.reshape(n_tiles, 1, tm)
    smem_blk = lambda fn: pl.BlockSpec((None, 1, tm), fn, memory_space=pltpu.SMEM)
    return pl.pallas_call(
        functools.partial(_ffn_body, tm, FF // FFN_TF), grid=(E, mt, FF // FFN_TF),
        in_specs=[
            smem_blk(lambda e, m, f: (e * mt + m, 0, 0)),
            smem_blk(lambda e, m, f: (jnp.minimum(e * mt + m + 1, n_tiles - 1), 0, 0)),
            pl.BlockSpec(memory_space=pl.ANY),
            pl.BlockSpec((None, tm, 1), lambda e, m, f: (e, m, 0)),
            pl.BlockSpec((None, None, D, FFN_TF), lambda e, m, f: (layer, e, 0, f)),
            pl.BlockSpec((None, None, D, FFN_TF), lambda e, m, f: (layer, e, 0, f)),
            pl.BlockSpec((None, None, FFN_TF, D), lambda e, m, f: (layer, e, f, 0)),
        ],
        out_specs=pl.BlockSpec((tm, SUBLANES, LANES), lambda e, m, f: (e * mt + m, 0, 0)),
        out_shape=jax.ShapeDtypeStruct((E * cap, SUBLANES, LANES), F32),
        scratch_shapes=[pltpu.VMEM((2, tm, D), F32), pltpu.VMEM((tm, D), BF16), pltpu.VMEM((tm, D), F32),
                        pltpu.SemaphoreType.DMA((2,))],
        name="ffn", compiler_params=_cparams(("arbitrary", "arbitrary", "arbitrary")))(
            idx3, idx3, x, gate_col, w_gate, w_up, w_down)


COMBINE_PASSES = 4
COMBINE_ROWS = 256
COMBINE_UNROLL = 8


def _combine_body(cap, q_tokens, n_rows_total, start_ref, idx_ref, y_ref, acc_ref, buf_ref, sem, cnt_ref):
    qi = pl.program_id(0)
    e = pl.program_id(1)
    n_q = pl.num_programs(0)
    n_e = pl.num_programs(1)
    first = jnp.logical_and(qi == 0, e == 0)

    @pl.when(first)
    def _():
        cnt_ref[0] = 0

    @pl.when(e == 0)
    def _():
        acc_ref[...] = jnp.zeros_like(acc_ref)

    def span(ee, qq):
        lo = start_ref[ee, qq]
        hi = start_ref[ee, qq + 1]
        return lo, hi, (hi - lo + COMBINE_ROWS - 1) // COMBINE_ROWS

    def chunk_src(ee, lo, ci):
        return jnp.minimum(ee * cap + lo + ci * COMBINE_ROWS, n_rows_total - COMBINE_ROWS)

    def copy(ee, lo, ci, slot):
        return pltpu.make_async_copy(y_ref.at[pl.ds(chunk_src(ee, lo, ci), COMBINE_ROWS)], buf_ref.at[slot],
                                     sem.at[slot])

    a, b, n_chunks = span(e, qi)
    tok0 = qi * q_tokens
    g0 = cnt_ref[0]

    @pl.when(jnp.logical_and(first, n_chunks > 0))
    def _():
        copy(e, a, 0, g0 % 2).start()

    def chunk(ci, carry):
        slot = (g0 + ci) % 2

        @pl.when(ci + 1 < n_chunks)
        def _():
            copy(e, a, ci + 1, 1 - slot).start()

        copy(e, a, ci, slot).wait()
        lo = a + ci * COMBINE_ROWS
        hi = jnp.minimum(lo + COMBINE_ROWS, b)
        shift = e * cap - chunk_src(e, a, ci)

        def add_rows(j0, n):
            ts = [idx_ref[0, j0 + u] - tok0 for u in range(n)]
            sums = [acc_ref[ts[u]] + buf_ref[slot, j0 + u + shift] for u in range(n)]
            for u in range(n):
                acc_ref[ts[u]] = sums[u]

        n_groups = (hi - lo) // COMBINE_UNROLL

        def add_group(g, c):
            add_rows(lo + g * COMBINE_UNROLL, COMBINE_UNROLL)
            return c

        def add_one(j, c):
            add_rows(j, 1)
            return c

        lax.fori_loop(0, n_groups, add_group, 0)
        lax.fori_loop(lo + n_groups * COMBINE_UNROLL, hi, add_one, 0)
        return carry

    lax.fori_loop(0, n_chunks, chunk, 0)
    g1 = g0 + n_chunks
    cnt_ref[0] = g1

    @pl.when(jnp.logical_not(jnp.logical_and(qi == n_q - 1, e == n_e - 1)))
    def _():
        wrap = e == n_e - 1
        e2 = jnp.where(wrap, 0, e + 1)
        q2 = jnp.where(wrap, qi + 1, qi)
        a2, _, n2 = span(e2, q2)

        @pl.when(n2 > 0)
        def _():
            copy(e2, a2, 0, g1 % 2).start()


def _combine(y_rows, idx, starts, S):
    E, cap = idx.shape
    q_tokens = S // COMBINE_PASSES
    n_rows_total = y_rows.shape[0]
    grid_spec = pltpu.PrefetchScalarGridSpec(
        num_scalar_prefetch=1, grid=(COMBINE_PASSES, E),
        in_specs=[pl.BlockSpec((None, 1, cap), lambda q, e, st: (e, 0, 0), memory_space=pltpu.SMEM),
                  pl.BlockSpec(memory_space=pl.ANY)],
        out_specs=pl.BlockSpec((q_tokens, SUBLANES, LANES), lambda q, e, st: (q, 0, 0)),
        scratch_shapes=[pltpu.VMEM((2, COMBINE_ROWS, SUBLANES, LANES), F32), pltpu.SemaphoreType.DMA((2,)),
                        pltpu.SMEM((1,), I32)])
    return pl.pallas_call(
        functools.partial(_combine_body, cap, q_tokens, n_rows_total), grid_spec=grid_spec,
        out_shape=jax.ShapeDtypeStruct((S, SUBLANES, LANES), F32), name="combine",
        compiler_params=_cparams(("arbitrary", "arbitrary")))(starts, idx.reshape(E, 1, cap), y_rows)


LN_TM = 512


def _ln_body(x_ref, f_ref, g_ref, b_ref, o_ref):
    o_ref[...] = _layer_norm(DN_ALPHA * x_ref[...] + f_ref[...], g_ref[...], b_ref[...])


def _residual_ln(x, f, g, b):
    S, D = x.shape
    row = pl.BlockSpec((LN_TM, D), lambda i: (i, 0))
    vec = pl.BlockSpec((1, D), lambda i: (0, 0))
    return pl.pallas_call(
        _ln_body, grid=(S // LN_TM,), in_specs=[row, row, vec, vec], out_specs=row,
        out_shape=jax.ShapeDtypeStruct((S, D), F32), name="ln",
        compiler_params=_cparams(("parallel",)))(x, f, g, b)


PROJ_TM = 512


def _pad_cols(w, n):
    return jnp.pad(w, ((0, 0), (0, n - w.shape[1])))


def _layer(x, p, experts, layer):
    S, D = x.shape
    w_in = p["w_in"].astype(BF16)
    c0 = ATT_Q + 2 * ATT_KV
    c1 = c0 + 3 * CONV_WIDTH
    c2 = c1 + 2 * GLA_KEY_DIM
    c3 = c2 + GLA_VALUE_DIM
    c4 = c3 + GLA_VALUE_DIM
    c5 = c4 + 2 * GLA_GATE_RANK
    w_qkv = jnp.concatenate([w_in[:, :ATT_Q] * (ATT_HEAD_DIM ** -0.5), w_in[:, ATT_Q:c0]], axis=1)
    ws = [w_qkv, w_in[:, c0:c1], w_in[:, c1:c2], w_in[:, c2:c3], w_in[:, c3:c4],
          _pad_cols(w_in[:, c4:c5], LANES)]
    qkv, conv, gqk, gv, gr, glr = _proj(x, ws, [BF16, F32, F32, F32, F32, F32], PROJ_TM, "proj_mix")
    (mg,) = _proj(x, [w_in[:, c5:]], [F32], PROJ_TM, "proj_gates")

    y_attn = _attention(qkv, p["attn_sink"])

    w2 = p["gla_gate_w2"].astype(BF16)
    rk = GLA_GATE_RANK
    w2_f = jnp.zeros((LANES, GLA_KEY_DIM), BF16).at[:rk].set(w2[0])
    w2_b = jnp.zeros((LANES, GLA_KEY_DIM), BF16).at[rk:2 * rk].set(w2[1])
    gb = p["gla_gate_b"].astype(F32)
    o_f = _gla_direction(gqk, gv, glr, w2_f, gb[0:1], False)
    o_b = _gla_direction(gqk, gv, glr, w2_b, gb[1:2], True)

    rw_t = p["router_w"].T.astype(F32)
    rw_hi = rw_t.astype(BF16)
    rw3 = jnp.concatenate([rw_hi, (rw_t - rw_hi.astype(F32)).astype(BF16)], axis=0)
    x1, aff_t = _merge(
        x, y_attn, conv, o_f, o_b, gr, mg,
        p["w_branch_attn"].astype(BF16), p["w_branch_conv"].astype(BF16), p["w_branch_gla"].astype(BF16),
        p["w_out"].astype(BF16), p["conv_w"].astype(F32), p["gla_norm_g"].astype(F32).reshape(1, -1),
        p["ln_mix_g"].astype(F32).reshape(1, -1), p["ln_mix_b"].astype(F32).reshape(1, -1), rw3)

    cap = CAPACITY_FACTOR * S // N_EXPERTS
    idx, gate, offs = _topk(aff_t, cap)
    y = _expert_ffn(x1, idx, gate.reshape(N_EXPERTS, cap, 1), experts["expert_w_gate"], experts["expert_w_up"],
                    experts["expert_w_down"], layer)
    rows_per_pass = S // COMBINE_PASSES // LANES
    starts = jnp.concatenate([offs[:, ::rows_per_pass], jnp.full((N_EXPERTS, 1), cap, I32)], axis=1)
    ffn = _combine(y, idx, starts, S).reshape(S, D)
    return _residual_ln(x1, ffn, p["ln_ffn_g"].astype(F32).reshape(1, -1), p["ln_ffn_b"].astype(F32).reshape(1, -1))


def kernel(x, w_in, attn_sink, conv_w, gla_gate_w2, gla_gate_b, gla_norm_g, w_branch_attn, w_branch_conv,
           w_branch_gla, w_out, ln_mix_g, ln_mix_b, router_w, expert_w_gate, expert_w_up, expert_w_down,
           ln_ffn_g, ln_ffn_b):
    params = dict(w_in=w_in, attn_sink=attn_sink, conv_w=conv_w, gla_gate_w2=gla_gate_w2, gla_gate_b=gla_gate_b,
                  gla_norm_g=gla_norm_g, w_branch_attn=w_branch_attn, w_branch_conv=w_branch_conv,
                  w_branch_gla=w_branch_gla, w_out=w_out, ln_mix_g=ln_mix_g, ln_mix_b=ln_mix_b,
                  router_w=router_w, expert_w_gate=expert_w_gate, expert_w_up=expert_w_up,
                  expert_w_down=expert_w_down, ln_ffn_g=ln_ffn_g, ln_ffn_b=ln_ffn_b)
    B, S, D = x.shape
    outs = []
    for bi in range(B):
        h = x[bi]
        for l in range(DEPTH):
            h = _layer(h, {name: val[l] for name, val in params.items() if not name.startswith("expert_")},
                       params, l)
        outs.append(h)
    return jnp.stack(outs)
```

```python
import functools
import math

import numpy as np
import jax
import jax.numpy as jnp
from jax import lax
from jax.experimental import pallas as pl
from jax.experimental.pallas import tpu as pltpu

F32 = jnp.float32
BF16 = jnp.bfloat16
I32 = jnp.int32

D_MODEL = 1024
DEPTH = 2
ATT_HEADS, ATT_KV_HEADS, ATT_HEAD_DIM, ATT_BLOCK, ATT_WINDOW = 8, 2, 64, 128, 128
ATT_GROUP = ATT_HEADS // ATT_KV_HEADS
CONV_WIDTH = 512
GLA_HEADS, GLA_KEY_DIM, GLA_VALUE_DIM, GLA_GATE_RANK, GLA_TAU = 4, 256, 512, 16, 16.0
GLA_DK = GLA_KEY_DIM // GLA_HEADS
GLA_DV = GLA_VALUE_DIM // GLA_HEADS
N_EXPERTS, EXPERT_FF, CAPACITY_FACTOR = 16, 2048, 2
DN_ALPHA = (2 * DEPTH) ** 0.25
LN_EPS = 1e-5
RMS_EPS = 1e-6
ATT_Q, ATT_KV = ATT_HEADS * ATT_HEAD_DIM, ATT_KV_HEADS * ATT_HEAD_DIM

LANES = 128
SUBLANES = 8
NEG_BIG = -1e30

GLA_CHUNK = 128
VMEM_LIMIT = 56 * 1024 * 1024


def _cparams(sem, vmem=VMEM_LIMIT):
    return pltpu.CompilerParams(dimension_semantics=sem, vmem_limit_bytes=vmem)


def _proj_body(n_out, x_ref, *refs):
    xb = x_ref[...].astype(BF16)
    for w_ref, o_ref in zip(refs[:n_out], refs[n_out:]):
        o_ref[...] = jnp.dot(xb, w_ref[...], preferred_element_type=F32).astype(o_ref.dtype)


def _proj(x, ws, out_dtypes, tm, name):
    S, D = x.shape
    in_specs = [pl.BlockSpec((tm, D), lambda i: (i, 0))]
    in_specs += [pl.BlockSpec(w.shape, lambda i: (0, 0)) for w in ws]
    out_specs = [pl.BlockSpec((tm, w.shape[1]), lambda i: (i, 0)) for w in ws]
    out_shape = [jax.ShapeDtypeStruct((S, w.shape[1]), dt) for w, dt in zip(ws, out_dtypes)]
    return pl.pallas_call(
        functools.partial(_proj_body, len(ws)), grid=(S // tm,), in_specs=in_specs,
        out_specs=out_specs, out_shape=out_shape, name=name,
        compiler_params=_cparams(("parallel",)))(x, *ws)


ATT_TQ = 4 * ATT_BLOCK


def _attn_bias_table():
    q_off = np.arange(ATT_BLOCK)[:, None]
    k_off = np.arange(3 * ATT_BLOCK)[None, :] - ATT_BLOCK
    dist = np.abs(q_off - k_off).astype(np.float32)
    slopes = 2.0 ** (-8.0 * np.arange(1, ATT_HEADS + 1, dtype=np.float32) / ATT_HEADS)
    return np.stack([np.where(dist <= ATT_WINDOW, -sl * dist, NEG_BIG) for sl in slopes]).astype(np.float32)


def _attn_body(n_steps, sink_ref, cur_ref, prev_ref, next_ref, bias_ref, o_ref):
    i = pl.program_id(0)
    hd = ATT_HEAD_DIM
    per = ATT_TQ // ATT_BLOCK
    kv_all = jnp.concatenate(
        [prev_ref[:, ATT_Q:], cur_ref[:, ATT_Q:], next_ref[:, ATT_Q:]], axis=0)
    col = lax.broadcasted_iota(I32, (1, 3 * ATT_BLOCK), 1)
    for b in range(per):
        rows = slice(b * ATT_BLOCK, (b + 1) * ATT_BLOCK)
        kvb = kv_all[b * ATT_BLOCK:(b + 3) * ATT_BLOCK]
        edge = None
        if b == 0:
            edge = jnp.where(jnp.logical_and(col < ATT_BLOCK, i == 0), NEG_BIG, 0.0).astype(F32)
        if b == per - 1:
            last = jnp.where(jnp.logical_and(col >= 2 * ATT_BLOCK, i == n_steps - 1), NEG_BIG, 0.0).astype(F32)
            edge = last if edge is None else edge + last
        scores = []
        for h in range(ATT_HEADS):
            j = h // ATT_GROUP
            q = cur_ref[rows, h * hd:(h + 1) * hd]
            kj = kvb[:, j * hd:(j + 1) * hd]
            s = lax.dot_general(q, kj, (((1,), (1,)), ((), ())), preferred_element_type=F32) + bias_ref[h]
            scores.append(s if edge is None else s + edge)
        probs, scales = [], []
        for h in range(ATT_HEADS):
            sink = sink_ref[h]
            m = jnp.maximum(jnp.max(scores[h], axis=-1, keepdims=True), sink)
            p = jnp.exp(scores[h] - m)
            denom = jnp.sum(p, axis=-1, keepdims=True) + jnp.exp(sink - m)
            probs.append(p.astype(BF16))
            scales.append(1.0 / denom)
        outs = []
        for h in range(ATT_HEADS):
            j = h // ATT_GROUP
            vj = kvb[:, ATT_KV + j * hd:ATT_KV + (j + 1) * hd]
            outs.append(jnp.dot(probs[h], vj, preferred_element_type=F32) * scales[h])
        o_ref[rows, :] = jnp.concatenate(outs, axis=-1).astype(o_ref.dtype)


def _attention(qkv, sink):
    S = qkv.shape[0]
    n_blocks = S // ATT_BLOCK
    per = ATT_TQ // ATT_BLOCK
    n_steps = S // ATT_TQ
    width = qkv.shape[1]
    bias = jnp.asarray(_attn_bias_table())
    grid_spec = pltpu.PrefetchScalarGridSpec(
        num_scalar_prefetch=1, grid=(n_steps,),
        in_specs=[
            pl.BlockSpec((ATT_TQ, width), lambda i, sk: (i, 0)),
            pl.BlockSpec((ATT_BLOCK, width), lambda i, sk: (jnp.maximum(i * per - 1, 0), 0)),
            pl.BlockSpec((ATT_BLOCK, width), lambda i, sk: (jnp.minimum((i + 1) * per, n_blocks - 1), 0)),
            pl.BlockSpec(bias.shape, lambda i, sk: (0, 0, 0)),
        ],
        out_specs=pl.BlockSpec((ATT_TQ, ATT_Q), lambda i, sk: (i, 0)))
    return pl.pallas_call(
        functools.partial(_attn_body, n_steps), grid_spec=grid_spec,
        out_shape=jax.ShapeDtypeStruct((S, ATT_Q), BF16), name="attn",
        compiler_params=_cparams(("parallel",)))(sink.astype(F32), qkv, qkv, qkv, bias)


GLA_TB = 512


def _mid_broadcast(x, w, pos, reverse):
    C, N = x.shape
    if 2 * w >= SUBLANES:
        x3 = x.reshape(C // (2 * w), 2 * w, N)
        r = w if reverse else w - 1
        return jnp.broadcast_to(x3[:, r:r + 1, :], x3.shape).reshape(C, N)
    out = jnp.zeros_like(x)
    for o in range(-(w - 1), w + 1):
        shift = (-o if reverse else o) % C
        src = x if shift == 0 else pltpu.roll(x, shift, 0)
        out = jnp.where(pos - (w - 1) == o, src, out)
    return out


def _gla_body(reverse, qk_ref, v_ref, lr_ref, w2_ref, gb_ref, o_ref, state_ref):
    C = GLA_CHUNK
    H, dk, dv = GLA_HEADS, GLA_DK, GLA_DV

    @pl.when(pl.program_id(0) == 0)
    def _():
        state_ref[...] = jnp.zeros_like(state_ref)

    row = lax.broadcasted_iota(I32, (C, 1), 0)
    rowi = lax.broadcasted_iota(I32, (C, C), 0)
    coli = lax.broadcasted_iota(I32, (C, C), 1)
    eye = rowi == coli

    def scan_pos(idx, w):
        p = idx & (2 * w - 1)
        return (2 * w - 1) - p if reverse else p

    levels = []
    w = 1
    while w < C:
        shift = int(math.log2(2 * w))
        pair = jnp.logical_and(jnp.right_shift(rowi, shift) == jnp.right_shift(coli, shift),
                               jnp.logical_and(scan_pos(rowi, w) >= w, scan_pos(coli, w) < w))
        pos = scan_pos(row, w)
        levels.append((w, pos, pos >= w, jnp.where(pair, 1.0, 0.0).astype(F32)))
        w *= 2

    def prepare(c):
        rows = slice(c * C, (c + 1) * C)
        q = qk_ref[rows, :GLA_KEY_DIM] * (dk ** -0.5)
        k = qk_ref[rows, GLA_KEY_DIM:]
        z = jnp.dot(lr_ref[rows, :].astype(BF16), w2_ref[...], preferred_element_type=F32) + gb_ref[...]
        la = (jnp.minimum(z, 0.0) - jnp.log(1.0 + jnp.exp(-jnp.abs(z)))) * (1.0 / GLA_TAU)
        cw = la
        scores = [None] * H
        for w, pos, second, pair in levels:
            mid = _mid_broadcast(cw, w, pos, reverse)
            fac = jnp.exp(jnp.where(second, cw, mid - cw))
            t = (jnp.where(second, q, k) * fac).astype(BF16)
            for h in range(H):
                th = t[:, h * dk:(h + 1) * dk]
                sc = lax.dot_general(th, th, (((1,), (1,)), ((), ())), preferred_element_type=F32)
                sc = sc * pair
                scores[h] = sc if scores[h] is None else scores[h] + sc
            cw = cw + jnp.where(second, mid, 0.0)
        b = cw
        btot = b[0:1, :] if reverse else b[C - 1:C, :]
        qhat = (q * jnp.exp(b)).astype(BF16)
        khat = (k * jnp.exp(btot - b)).astype(BF16)
        qk_diag = q * k
        a_mats = []
        for h in range(H):
            diag = jnp.sum(qk_diag[:, h * dk:(h + 1) * dk], axis=-1, keepdims=True)
            a_mats.append((scores[h] + jnp.where(eye, diag, 0.0)).astype(BF16))
        return rows, a_mats, qhat, khat, jnp.exp(btot)

    def finish(prep):
        rows, a_mats, qhat, khat, a_tot = prep
        v = v_ref[rows, :]
        for h in range(H):
            ksl = slice(h * dk, (h + 1) * dk)
            vsl = slice(h * dv, (h + 1) * dv)
            vh = v[:, vsl]
            st = state_ref[h]
            o_h = jnp.dot(a_mats[h], vh.astype(BF16), preferred_element_type=F32)
            o_h = o_h + lax.dot_general(qhat[:, ksl], st.astype(BF16), (((1,), (1,)), ((), ())),
                                        preferred_element_type=F32)
            o_ref[rows, vsl] = o_h
            upd = jnp.dot(vh.T.astype(BF16), khat[:, ksl], preferred_element_type=F32)
            state_ref[h] = st * a_tot[:, ksl] + upd

    n_chunks = GLA_TB // C
    order = list(range(n_chunks - 1, -1, -1) if reverse else range(n_chunks))
    prep = prepare(order[0])
    for nxt in order[1:]:
        prep_next = prepare(nxt)
        finish(prep)
        prep = prep_next
    finish(prep)


def _gla_direction(qk, v, lr, w2p, gb, reverse):
    S = qk.shape[0]
    n = S // GLA_TB
    blk = (lambda i: (n - 1 - i, 0)) if reverse else (lambda i: (i, 0))
    return pl.pallas_call(
        functools.partial(_gla_body, reverse), grid=(n,),
        in_specs=[
            pl.BlockSpec((GLA_TB, 2 * GLA_KEY_DIM), blk),
            pl.BlockSpec((GLA_TB, GLA_VALUE_DIM), blk),
            pl.BlockSpec((GLA_TB, LANES), blk),
            pl.BlockSpec(w2p.shape, lambda i: (0, 0)),
            pl.BlockSpec(gb.shape, lambda i: (0, 0)),
        ],
        out_specs=pl.BlockSpec((GLA_TB, GLA_VALUE_DIM), blk),
        out_shape=jax.ShapeDtypeStruct((S, GLA_VALUE_DIM), F32),
        scratch_shapes=[pltpu.VMEM((GLA_HEADS, GLA_DV, GLA_DK), F32)], name="gla",
        compiler_params=_cparams(("arbitrary",)))(qk, v, lr, w2p, gb)


MERGE_TM = 512


def _layer_norm(h, g, b):
    mu = jnp.mean(h, axis=-1, keepdims=True)
    d = h - mu
    var = jnp.mean(d * d, axis=-1, keepdims=True)
    return d * lax.rsqrt(var + LN_EPS) * g + b


def _sigmoid(x):
    return 0.5 * jnp.tanh(0.5 * x) + 0.5


def _split3(x):
    hi = x.astype(BF16)
    r1 = x - hi.astype(F32)
    mid = r1.astype(BF16)
    lo = (r1 - mid.astype(F32)).astype(BF16)
    return hi, mid, lo


def _merge_body(n_tiles, x_ref, ya_ref, cv_ref, cvp_ref, cvn_ref, of_ref, ob_ref, r_ref, mg_ref,
                wa_ref, wc_ref, wg_ref, wo_ref, cw_ref, ng_ref, lg_ref, lb_ref, rw_ref,
                x1_ref, aff_ref):
    i = pl.program_id(0)
    T = MERGE_TM
    CW = CONV_WIDTH
    u = cv_ref[:, 2 * CW:] * cv_ref[:, :CW]
    u_before = cvp_ref[SUBLANES - 1:SUBLANES, 2 * CW:] * cvp_ref[SUBLANES - 1:SUBLANES, :CW]
    u_after = cvn_ref[0:1, 2 * CW:] * cvn_ref[0:1, :CW]
    u_before = jnp.where(i == 0, 0.0, u_before)
    u_after = jnp.where(i == n_tiles - 1, 0.0, u_after)
    row = lax.broadcasted_iota(I32, (T, 1), 0)
    u_m1 = jnp.where(row == 0, u_before, pltpu.roll(u, 1, 0))
    u_p1 = jnp.where(row == T - 1, u_after, pltpu.roll(u, T - 1, 0))
    y_conv = cv_ref[:, CW:2 * CW] * (cw_ref[0:1, :] * u_m1 + cw_ref[1:2, :] * u + cw_ref[2:3, :] * u_p1)

    o = of_ref[...] + ob_ref[...]
    parts = []
    for h in range(GLA_HEADS):
        oh = o[:, h * GLA_DV:(h + 1) * GLA_DV]
        parts.append(oh * lax.rsqrt(jnp.mean(oh * oh, axis=-1, keepdims=True) + RMS_EPS))
    r = r_ref[...]
    y_gla = jnp.concatenate(parts, axis=-1) * ng_ref[...] * (r * _sigmoid(r))

    D = D_MODEL
    merged = _sigmoid(mg_ref[:, :D]) * jnp.dot(ya_ref[...], wa_ref[...], preferred_element_type=F32)
    merged += _sigmoid(mg_ref[:, D:2 * D]) * jnp.dot(y_conv.astype(BF16), wc_ref[...], preferred_element_type=F32)
    merged += _sigmoid(mg_ref[:, 2 * D:]) * jnp.dot(y_gla.astype(BF16), wg_ref[...], preferred_element_type=F32)
    hres = DN_ALPHA * x_ref[...] + jnp.dot(merged.astype(BF16), wo_ref[...], preferred_element_type=F32)
    x1 = _layer_norm(hres, lg_ref[...], lb_ref[...])
    x1_ref[...] = x1

    nt = (((1,), (1,)), ((), ()))
    xh = x1.astype(BF16)
    xl = (x1 - xh.astype(F32)).astype(BF16)
    both = lax.dot_general(rw_ref[...], xh, nt, preferred_element_type=F32)
    logits = (both[:N_EXPERTS] + both[N_EXPERTS:]
              + lax.dot_general(rw_ref[:N_EXPERTS], xl, nt, preferred_element_type=F32))
    mx = jnp.max(logits, axis=0, keepdims=True)
    ex = jnp.exp(logits - mx)
    aff_ref[...] = ex / jnp.sum(ex, axis=0, keepdims=True)


def _merge(x, y_attn, conv, o_f, o_b, r, mg, wa, wc, wg, wo, cw, ng, lg, lb, rw3):
    S, D = x.shape
    T = MERGE_TM
    n_tiles = S // T
    per = T // SUBLANES
    n8 = S // SUBLANES
    row_spec = lambda wd: pl.BlockSpec((T, wd), lambda i: (i, 0))
    full = lambda a: pl.BlockSpec(a.shape, lambda i: (0,) * a.ndim)
    return pl.pallas_call(
        functools.partial(_merge_body, n_tiles), grid=(n_tiles,),
        in_specs=[
            row_spec(D), row_spec(ATT_Q), row_spec(3 * CONV_WIDTH),
            pl.BlockSpec((SUBLANES, 3 * CONV_WIDTH), lambda i: (jnp.maximum(i * per - 1, 0), 0)),
            pl.BlockSpec((SUBLANES, 3 * CONV_WIDTH), lambda i: (jnp.minimum((i + 1) * per, n8 - 1), 0)),
            row_spec(GLA_VALUE_DIM), row_spec(GLA_VALUE_DIM), row_spec(GLA_VALUE_DIM), row_spec(3 * D),
            full(wa), full(wc), full(wg), full(wo), full(cw), full(ng), full(lg), full(lb), full(rw3),
        ],
        out_specs=[row_spec(D), pl.BlockSpec((N_EXPERTS, T), lambda i: (0, i))],
        out_shape=[jax.ShapeDtypeStruct((S, D), F32), jax.ShapeDtypeStruct((N_EXPERTS, S), F32)],
        name="merge", compiler_params=_cparams(("parallel",)))(
            x, y_attn, conv, conv, conv, o_f, o_b, r, mg, wa, wc, wg, wo, cw, ng, lg, lb, rw3)


def _thr_body(k, aff_ref, thr_ref):
    a = aff_ref[...]

    def step(t, thr):
        cand = thr | jnp.left_shift(jnp.int32(1), 30 - t)
        ge = (a >= pltpu.bitcast(cand, F32)).astype(I32)
        cnt = jnp.sum(jnp.sum(ge, axis=1, keepdims=True), axis=2, keepdims=True)
        return jnp.where(cnt >= k, cand, thr)

    thr = lax.fori_loop(0, 31, step, jnp.zeros((N_EXPERTS, 1, 1), I32))
    thr_ref[...] = jnp.broadcast_to(thr, thr_ref.shape)


def _select_body(k, aff_ref, thr_ref, idx_ref, gate_ref, offs_ref):
    R = aff_ref.shape[0]
    L = LANES
    a = aff_ref[...]
    thr_bits = thr_ref[...]
    thr = pltpu.bitcast(thr_bits, F32)[0:1, 0:1]
    thr_next = pltpu.bitcast(thr_bits + 1, F32)[0:1, 0:1]
    gt = a >= thr_next
    eq = jnp.logical_and(a >= thr, jnp.logical_not(gt))
    n_gt = jnp.sum(jnp.sum(gt.astype(F32), axis=0, keepdims=True), axis=1, keepdims=True)
    need = float(k) - n_gt

    li = lax.broadcasted_iota(I32, (L, L), 0)
    lj = lax.broadcasted_iota(I32, (L, L), 1)
    upper = jnp.where(li <= lj, 1.0, 0.0).astype(BF16)
    ri = lax.broadcasted_iota(I32, (R, R), 0)
    rj = lax.broadcasted_iota(I32, (R, R), 1)
    lower_strict = jnp.where(rj < ri, 1.0, 0.0).astype(BF16)

    def prefix(mask_f):
        cum = jnp.dot(mask_f.astype(BF16), upper, preferred_element_type=F32)
        tot = jnp.broadcast_to(cum[:, L - 1:L], (R, L))
        offs = jnp.dot(lower_strict, tot.astype(BF16), preferred_element_type=F32)
        return cum, tot, offs

    eq_f = eq.astype(F32)
    cum_eq, _, offs_eq = prefix(eq_f)
    rank_eq = cum_eq - eq_f + offs_eq
    sel = jnp.logical_or(gt, jnp.logical_and(eq, rank_eq < need))
    sel_f = sel.astype(F32)
    cum, tot, offs = prefix(sel_f)
    offs_ref[...] = offs.astype(I32)

    offs_incl = offs + tot
    cum_t = cum.T.astype(BF16)
    a_t = a.T
    ah, am, al = _split3(a_t)
    row_iota = lax.broadcasted_iota(I32, (R, L), 0).astype(F32)
    lane_iota_col = lax.broadcasted_iota(I32, (L, L), 0).astype(F32)
    slot_lane = lax.broadcasted_iota(I32, (1, L), 1).astype(F32)

    def chunk(ci, carry):
        slot = slot_lane + lax.convert_element_type(ci * L, F32)
        row_of = jnp.sum((offs_incl <= slot).astype(F32), axis=0, keepdims=True)
        onehot = (row_iota == row_of).astype(F32)
        base = jnp.sum(onehot * offs, axis=0, keepdims=True)
        local = slot - base
        oh = onehot.astype(BF16)
        cum_at = jnp.dot(cum_t, oh, preferred_element_type=F32)
        lane_of = jnp.sum((cum_at <= local).astype(F32), axis=0, keepdims=True)
        val = (jnp.dot(ah, oh, preferred_element_type=F32) + jnp.dot(am, oh, preferred_element_type=F32)
               + jnp.dot(al, oh, preferred_element_type=F32))
        gate = jnp.sum(jnp.where(lane_iota_col == lane_of, val, 0.0), axis=0, keepdims=True)
        idx_ref[pl.ds(ci, 1), :] = (row_of * L + lane_of).astype(I32)
        gate_ref[pl.ds(ci, 1), :] = gate
        return carry

    lax.fori_loop(0, idx_ref.shape[0], chunk, 0)


def _topk(aff_t, k):
    E, S = aff_t.shape
    R = S // LANES
    aff3 = aff_t.reshape(E, R, LANES)
    thr = pl.pallas_call(
        functools.partial(_thr_body, k), grid=(1,),
        in_specs=[pl.BlockSpec((E, R, LANES), lambda i: (0, 0, 0))],
        out_specs=pl.BlockSpec((E, SUBLANES, LANES), lambda i: (0, 0, 0)),
        out_shape=jax.ShapeDtypeStruct((E, SUBLANES, LANES), I32), name="topk_thr",
        compiler_params=_cparams(("arbitrary",)))(aff3)
    n_chunks = k // LANES
    idx, gate, offs = pl.pallas_call(
        functools.partial(_select_body, k), grid=(E,),
        in_specs=[pl.BlockSpec((None, R, LANES), lambda e: (e, 0, 0)),
                  pl.BlockSpec((None, SUBLANES, LANES), lambda e: (e, 0, 0))],
        out_specs=[pl.BlockSpec((None, n_chunks, LANES), lambda e: (e, 0, 0)),
                   pl.BlockSpec((None, n_chunks, LANES), lambda e: (e, 0, 0)),
                   pl.BlockSpec((None, R, LANES), lambda e: (e, 0, 0))],
        out_shape=[jax.ShapeDtypeStruct((E, n_chunks, LANES), I32),
                   jax.ShapeDtypeStruct((E, n_chunks, LANES), F32),
                   jax.ShapeDtypeStruct((E, R, LANES), I32)],
        name="topk_select", compiler_params=_cparams(("parallel",)))(aff3, thr)
    return idx.reshape(E, k), gate.reshape(E, k), offs[:, :, 0]


FFN_TM = 1024
FFN_TF = 512


def _ffn_body(tm, n_f, idx_ref, idx_next_ref, x_hbm, gate_ref, wg_ref, wu_ref, wd_ref, y_ref, xs_ref, xb_ref,
              acc_ref, sem):
    f = pl.program_id(2)
    n_tiles = pl.num_programs(0) * pl.num_programs(1)
    tile = pl.program_id(0) * pl.num_programs(1) + pl.program_id(1)
    slot = tile % 2
    per_step = tm // n_f

    def row_copy(ids, j, s):
        return pltpu.make_async_copy(x_hbm.at[pl.ds(ids[0, j], 1), :], xs_ref.at[s, pl.ds(j, 1), :], sem.at[s])

    def wait_tile(s):
        pltpu.make_async_copy(x_hbm.at[pl.ds(0, tm), :], xs_ref.at[s], sem.at[s]).wait()

    @pl.when(f == 0)
    def _():
        @pl.when(tile == 0)
        def _():
            def body(j, c):
                row_copy(idx_ref, j, slot).start()
                return c
            lax.fori_loop(0, tm, body, 0)

        wait_tile(slot)
        xb_ref[...] = xs_ref[slot].astype(BF16)

    for j in range(per_step):
        row_copy(idx_next_ref, f * per_step + j, 1 - slot).start()

    xb = xb_ref[...]
    hg = jnp.dot(xb, wg_ref[...].astype(BF16), preferred_element_type=F32)
    hu = jnp.dot(xb, wu_ref[...].astype(BF16), preferred_element_type=F32)
    hh = (hg * jax.nn.sigmoid(hg) * hu).astype(BF16)
    part = jnp.dot(hh, wd_ref[...].astype(BF16), preferred_element_type=F32)

    @pl.when(f == 0)
    def _():
        acc_ref[...] = part

    @pl.when(f > 0)
    def _():
        acc_ref[...] += part

    @pl.when(f == n_f - 1)
    def _():
        y_ref[...] = pltpu.einshape("m(sl)->msl", acc_ref[...] * gate_ref[...], s=SUBLANES, l=LANES)

        @pl.when(tile == n_tiles - 1)
        def _():
            wait_tile(1 - slot)


def _expert_ffn(x, idx, gate_col, w_gate, w_up, w_down, layer):
    E, cap = idx.shape
    D = x.shape[1]
    FF = w_gate.shape[-1]
    tm = min(FFN_TM, cap)
    mt = cap // tm
    n_tiles = E * mt
    idx3 = idx.reshape(n_tiles, 1, tm)
    smem_blk = lambda fn: pl.BlockSpec((None, 1, tm), fn, memory_space=pltpu.SMEM)
    return pl.pallas_call(
        functools.partial(_ffn_body, tm, FF // FFN_TF), grid=(E, mt, FF // FFN_TF),
        in_specs=[
            smem_blk(lambda e, m, f: (e * mt + m, 0, 0)),
            smem_blk(lambda e, m, f: (jnp.minimum(e * mt + m + 1, n_tiles - 1), 0, 0)),
            pl.BlockSpec(memory_space=pl.ANY),
            pl.BlockSpec((None, tm, 1), lambda e, m, f: (e, m, 0)),
            pl.BlockSpec((None, None, D, FFN_TF), lambda e, m, f: (layer, e, 0, f)),
            pl.BlockSpec((None, None, D, FFN_TF), lambda e, m, f: (layer, e, 0, f)),
            pl.BlockSpec((None, None, FFN_TF, D), lambda e, m, f: (layer, e, f, 0)),
        ],
        out_specs=pl.BlockSpec((tm, SUBLANES, LANES), lambda e, m, f: (e * mt + m, 0, 0)),
        out_shape=jax.ShapeDtypeStruct((E * cap, SUBLANES, LANES), F32),
        scratch_shapes=[pltpu.VMEM((2, tm, D), F32), pltpu.VMEM((tm, D), BF16), pltpu.VMEM((tm, D), F32),
                        pltpu.SemaphoreType.DMA((2,))],
        name="ffn", compiler_params=_cparams(("arbitrary", "arbitrary", "arbitrary")))(
            idx3, idx3, x, gate_col, w_gate, w_up, w_down)


COMBINE_PASSES = 4
COMBINE_ROWS = 256
COMBINE_UNROLL = 8


def _combine_body(cap, q_tokens, n_rows_total, start_ref, idx_ref, y_ref, acc_ref, buf_ref, sem, cnt_ref):
    qi = pl.program_id(0)
    e = pl.program_id(1)
    n_q = pl.num_programs(0)
    n_e = pl.num_programs(1)
    first = jnp.logical_and(qi == 0, e == 0)

    @pl.when(first)
    def _():
        cnt_ref[0] = 0

    @pl.when(e == 0)
    def _():
        acc_ref[...] = jnp.zeros_like(acc_ref)

    def span(ee, qq):
        lo = start_ref[ee, qq]
        hi = start_ref[ee, qq + 1]
        return lo, hi, (hi - lo + COMBINE_ROWS - 1) // COMBINE_ROWS

    def chunk_src(ee, lo, ci):
        return jnp.minimum(ee * cap + lo + ci * COMBINE_ROWS, n_rows_total - COMBINE_ROWS)

    def copy(ee, lo, ci, slot):
        return pltpu.make_async_copy(y_ref.at[pl.ds(chunk_src(ee, lo, ci), COMBINE_ROWS)], buf_ref.at[slot],
                                     sem.at[slot])

    a, b, n_chunks = span(e, qi)
    tok0 = qi * q_tokens
    g0 = cnt_ref[0]

    @pl.when(jnp.logical_and(first, n_chunks > 0))
    def _():
        copy(e, a, 0, g0 % 2).start()

    def chunk(ci, carry):
        slot = (g0 + ci) % 2

        @pl.when(ci + 1 < n_chunks)
        def _():
            copy(e, a, ci + 1, 1 - slot).start()

        copy(e, a, ci, slot).wait()
        lo = a + ci * COMBINE_ROWS
        hi = jnp.minimum(lo + COMBINE_ROWS, b)
        shift = e * cap - chunk_src(e, a, ci)

        def add_rows(j0, n):
            ts = [idx_ref[0, j0 + u] - tok0 for u in range(n)]
            sums = [acc_ref[ts[u]] + buf_ref[slot, j0 + u + shift] for u in range(n)]
            for u in range(n):
                acc_ref[ts[u]] = sums[u]

        n_groups = (hi - lo) // COMBINE_UNROLL

        def add_group(g, c):
            add_rows(lo + g * COMBINE_UNROLL, COMBINE_UNROLL)
            return c

        def add_one(j, c):
            add_rows(j, 1)
            return c

        lax.fori_loop(0, n_groups, add_group, 0)
        lax.fori_loop(lo + n_groups * COMBINE_UNROLL, hi, add_one, 0)
        return carry

    lax.fori_loop(0, n_chunks, chunk, 0)
    g1 = g0 + n_chunks
    cnt_ref[0] = g1

    @pl.when(jnp.logical_not(jnp.logical_and(qi == n_q - 1, e == n_e - 1)))
    def _():
        wrap = e == n_e - 1
        e2 = jnp.where(wrap, 0, e + 1)
        q2 = jnp.where(wrap, qi + 1, qi)
        a2, _, n2 = span(e2, q2)

        @pl.when(n2 > 0)
        def _():
            copy(e2, a2, 0, g1 % 2).start()


def _combine(y_rows, idx, starts, S):
    E, cap = idx.shape
    q_tokens = S // COMBINE_PASSES
    n_rows_total = y_rows.shape[0]
    grid_spec = pltpu.PrefetchScalarGridSpec(
        num_scalar_prefetch=1, grid=(COMBINE_PASSES, E),
        in_specs=[pl.BlockSpec((None, 1, cap), lambda q, e, st: (e, 0, 0), memory_space=pltpu.SMEM),
                  pl.BlockSpec(memory_space=pl.ANY)],
        out_specs=pl.BlockSpec((q_tokens, SUBLANES, LANES), lambda q, e, st: (q, 0, 0)),
        scratch_shapes=[pltpu.VMEM((2, COMBINE_ROWS, SUBLANES, LANES), F32), pltpu.SemaphoreType.DMA((2,)),
                        pltpu.SMEM((1,), I32)])
    return pl.pallas_call(
        functools.partial(_combine_body, cap, q_tokens, n_rows_total), grid_spec=grid_spec,
        out_shape=jax.ShapeDtypeStruct((S, SUBLANES, LANES), F32), name="combine",
        compiler_params=_cparams(("arbitrary", "arbitrary")))(starts, idx.reshape(E, 1, cap), y_rows)


LN_TM = 512


def _ln_body(x_ref, f_ref, g_ref, b_ref, o_ref):
    f = pltpu.einshape("msl->m(sl)", f_ref[...])
    o_ref[...] = _layer_norm(DN_ALPHA * x_ref[...] + f, g_ref[...], b_ref[...])


def _residual_ln(x, f, g, b):
    S, D = x.shape
    row = pl.BlockSpec((LN_TM, D), lambda i: (i, 0))
    tiles = pl.BlockSpec((LN_TM, SUBLANES, LANES), lambda i: (i, 0, 0))
    vec = pl.BlockSpec((1, D), lambda i: (0, 0))
    return pl.pallas_call(
        _ln_body, grid=(S // LN_TM,), in_specs=[row, tiles, vec, vec], out_specs=row,
        out_shape=jax.ShapeDtypeStruct((S, D), F32), name="ln",
        compiler_params=_cparams(("parallel",)))(x, f, g, b)


PROJ_TM = 512


def _pad_cols(w, n):
    return jnp.pad(w, ((0, 0), (0, n - w.shape[1])))


def _layer(x, p, experts, layer):
    S, D = x.shape
    w_in = p["w_in"].astype(BF16)
    c0 = ATT_Q + 2 * ATT_KV
    c1 = c0 + 3 * CONV_WIDTH
    c2 = c1 + 2 * GLA_KEY_DIM
    c3 = c2 + GLA_VALUE_DIM
    c4 = c3 + GLA_VALUE_DIM
    c5 = c4 + 2 * GLA_GATE_RANK
    w_qkv = jnp.concatenate([w_in[:, :ATT_Q] * (ATT_HEAD_DIM ** -0.5), w_in[:, ATT_Q:c0]], axis=1)
    ws = [w_qkv, w_in[:, c0:c1], w_in[:, c1:c2], w_in[:, c2:c3], w_in[:, c3:c4],
          _pad_cols(w_in[:, c4:c5], LANES)]
    qkv, conv, gqk, gv, gr, glr = _proj(x, ws, [BF16, F32, F32, F32, F32, F32], PROJ_TM, "proj_mix")
    (mg,) = _proj(x, [w_in[:, c5:]], [F32], PROJ_TM, "proj_gates")

    y_attn = _attention(qkv, p["attn_sink"])

    w2 = p["gla_gate_w2"].astype(BF16)
    rk = GLA_GATE_RANK
    w2_f = jnp.zeros((LANES, GLA_KEY_DIM), BF16).at[:rk].set(w2[0])
    w2_b = jnp.zeros((LANES, GLA_KEY_DIM), BF16).at[rk:2 * rk].set(w2[1])
    gb = p["gla_gate_b"].astype(F32)
    o_f = _gla_direction(gqk, gv, glr, w2_f, gb[0:1], False)
    o_b = _gla_direction(gqk, gv, glr, w2_b, gb[1:2], True)

    rw_t = p["router_w"].T.astype(F32)
    rw_hi = rw_t.astype(BF16)
    rw3 = jnp.concatenate([rw_hi, (rw_t - rw_hi.astype(F32)).astype(BF16)], axis=0)
    x1, aff_t = _merge(
        x, y_attn, conv, o_f, o_b, gr, mg,
        p["w_branch_attn"].astype(BF16), p["w_branch_conv"].astype(BF16), p["w_branch_gla"].astype(BF16),
        p["w_out"].astype(BF16), p["conv_w"].astype(F32), p["gla_norm_g"].astype(F32).reshape(1, -1),
        p["ln_mix_g"].astype(F32).reshape(1, -1), p["ln_mix_b"].astype(F32).reshape(1, -1), rw3)

    cap = CAPACITY_FACTOR * S // N_EXPERTS
    idx, gate, offs = _topk(aff_t, cap)
    y = _expert_ffn(x1, idx, gate.reshape(N_EXPERTS, cap, 1), experts["expert_w_gate"], experts["expert_w_up"],
                    experts["expert_w_down"], layer)
    rows_per_pass = S // COMBINE_PASSES // LANES
    starts = jnp.concatenate([offs[:, ::rows_per_pass], jnp.full((N_EXPERTS, 1), cap, I32)], axis=1)
    ffn = _combine(y, idx, starts, S)
    return _residual_ln(x1, ffn, p["ln_ffn_g"].astype(F32).reshape(1, -1), p["ln_ffn_b"].astype(F32).reshape(1, -1))


def kernel(x, w_in, attn_sink, conv_w, gla_gate_w2, gla_gate_b, gla_norm_g, w_branch_attn, w_branch_conv,
           w_branch_gla, w_out, ln_mix_g, ln_mix_b, router_w, expert_w_gate, expert_w_up, expert_w_down,
           ln_ffn_g, ln_ffn_b):
    params = dict(w_in=w_in, attn_sink=attn_sink, conv_w=conv_w, gla_gate_w2=gla_gate_w2, gla_gate_b=gla_gate_b,
                  gla_norm_g=gla_norm_g, w_branch_attn=w_branch_attn, w_branch_conv=w_branch_conv,
                  w_branch_gla=w_branch_gla, w_out=w_out, ln_mix_g=ln_mix_g, ln_mix_b=ln_mix_b,
                  router_w=router_w, expert_w_gate=expert_w_gate, expert_w_up=expert_w_up,
                  expert_w_down=expert_w_down, ln_ffn_g=ln_ffn_g, ln_ffn_b=ln_ffn_b)
    B, S, D = x.shape
    outs = []
    for bi in range(B):
        h = x[bi]
        for l in range(DEPTH):
            h = _layer(h, {name: val[l] for name, val in params.items() if not name.startswith("expert_")},
                       params, l)
        outs.append(h)
    return jnp.stack(outs)
```
